```python
import jax, jax.numpy as jnp
from jax import lax
import numpy as np

D_MODEL = 1024
BATCH = 8
SEQ = 2048
DEPTH = 4
DEC_BATCH = 128
DEC_SEQ = 4
PAST_LEN = 2048
PAGE_SIZE = 128

N_MIXERS = 2
N_ATTN_LAYERS = (DEPTH + 1) // 2
N_REC_LAYERS = DEPTH // 2
N_HEADS = 16
HEAD_DIM = D_MODEL // N_HEADS
KV_GROUPS = 2
HEADS_PER_GROUP = N_HEADS // KV_GROUPS
CMP_BLOCK = 32
CMP_STRIDE = 16
CMP_HIDDEN = 256
SEL_BLOCK = 64
SEL_TOPK = 16
WINDOW = 512
Q_BLOCK = 64
NSA_Q_COLS = N_HEADS * HEAD_DIM
NSA_KV_COLS = 6 * KV_GROUPS * HEAD_DIM
NSA_IN_COLS = NSA_Q_COLS + NSA_KV_COLS + 3 * N_HEADS
ATTN_SCALE = HEAD_DIM ** -0.5
FORCE_SCORE = 1e6
NEG_INF = -1e30
D_RNN = D_MODEL
RG_BLOCKS = 4
RG_BLOCK_DIM = D_RNN // RG_BLOCKS
RG_CONV = 4
RG_C = 8.0
D_FF = 3 * D_MODEL
FFN_CONV = 3
D_PLE = 256
ALPHA = (2 * DEPTH) ** 0.25
BETA = (8 * DEPTH) ** -0.25
LN_EPS = 1e-5

kernel_name = 'nsa_rglru_convffn_hybrid_step'


def layer_norm(x, g, b):
    xf = x.astype(jnp.float32)
    mu = jnp.mean(xf, axis=-1, keepdims=True)
    var = jnp.mean(jnp.square(xf - mu), axis=-1, keepdims=True)
    y = (xf - mu) * lax.rsqrt(var + LN_EPS) * g.astype(jnp.float32) + b.astype(jnp.float32)
    return y.astype(x.dtype)


def masked_softmax(s, mask):
    s = jnp.where(mask, s, NEG_INF)
    m = jnp.max(s, axis=-1, keepdims=True)
    e = jnp.where(mask, jnp.exp(s - m), 0.0)
    return e / jnp.maximum(jnp.sum(e, axis=-1, keepdims=True), 1e-30)


def causal_dwconv(u, w, b, prev):
    K = w.shape[0]
    T = u.shape[1]
    ext = jnp.concatenate([prev.astype(u.dtype), u], axis=1)
    out = b + ext[:, 0:T] * w[0]
    for k in range(1, K):
        out = out + ext[:, k:k + T] * w[k]
    return out, ext[:, T:]


def nsa_project(x, w_in):
    B, T, _ = x.shape
    proj = x @ w_in
    q = proj[..., :NSA_Q_COLS].reshape(B, T, KV_GROUPS, HEADS_PER_GROUP, HEAD_DIM)
    kv = proj[..., NSA_Q_COLS:NSA_Q_COLS + NSA_KV_COLS].reshape(B, T, 6, KV_GROUPS, HEAD_DIM)
    gates = jax.nn.sigmoid(proj[..., NSA_Q_COLS + NSA_KV_COLS:]).reshape(B, T, KV_GROUPS, HEADS_PER_GROUP, 3)
    return q, kv, gates


def compress(rows, pe, w1, b1, w2):
    nc = (rows.shape[1] - CMP_BLOCK) // CMP_STRIDE + 1
    idx = jnp.arange(nc)[:, None] * CMP_STRIDE + jnp.arange(CMP_BLOCK)[None, :]
    blk = rows[:, idx] + pe[:, None, :]
    hid = jax.nn.gelu(jnp.einsum('bclgd,ldh->bcgh', blk, w1) + b1)
    return jnp.einsum('bcgh,hd->bcgd', hid, w2)


def nsa_keys(rows, cmp):
    pe, w1, b1, w2 = cmp
    B, Tk = rows.shape[:2]
    kc = compress(rows[:, :, 0], pe[0], w1[0], b1[0], w2[0])
    vc = compress(rows[:, :, 1], pe[1], w1[1], b1[1], w2[1])
    nc = kc.shape[1]
    c_start = jnp.arange(nc) * CMP_STRIDE
    c_end = c_start + (CMP_BLOCK - 1)
    ns = -(-Tk // SEL_BLOCK)
    b_start = jnp.arange(ns) * SEL_BLOCK
    overlap = ((c_start[:, None] < b_start[None, :] + SEL_BLOCK)
               & (c_end[:, None] >= b_start[None, :])).astype(jnp.float32)
    sel = jnp.pad(rows[:, :, 2:4], ((0, 0), (0, ns * SEL_BLOCK - Tk), (0, 0), (0, 0), (0, 0)))
    sel = sel.reshape(B, ns, SEL_BLOCK, 2, KV_GROUPS, HEAD_DIM).transpose(3, 0, 4, 1, 2, 5)
    return (kc, vc, c_end, sel[0], sel[1], overlap)


def nsa_attend(q, gates, q_pos, ctx, kw, vw, w_pos):
    kc, vc, c_end, ks, vs, overlap = ctx
    B, Q = q.shape[:2]
    ns = ks.shape[2]
    f32 = jnp.float32
    s_c = jnp.einsum('bqghd,bcgd->bghqc', q, kc).astype(f32) * ATTN_SCALE
    p_c = masked_softmax(s_c, c_end[None, :] <= q_pos[:, None])
    o_c = jnp.einsum('bghqc,bcgd->bqghd', p_c.astype(vc.dtype), vc)
    imp = jnp.einsum('bghqc,cn->bgqn', p_c, overlap)
    blk = jnp.arange(ns)[None, :]
    cur = (q_pos // SEL_BLOCK)[:, None]
    imp = jnp.where((blk == 0) | (blk == cur) | (blk == cur - 1), FORCE_SCORE, imp)
    imp = jnp.where(blk > cur, -1.0, imp)
    n_top = min(SEL_TOPK, ns)
    _, idx = lax.top_k(imp, n_top)
    bi = jnp.arange(B)[:, None, None, None]
    gi = jnp.arange(KV_GROUPS)[None, :, None, None]
    k_sel = ks[bi, gi, idx].reshape(B, KV_GROUPS, Q, n_top * SEL_BLOCK, HEAD_DIM)
    v_sel = vs[bi, gi, idx].reshape(B, KV_GROUPS, Q, n_top * SEL_BLOCK, HEAD_DIM)
    s_pos = (idx[..., None] * SEL_BLOCK + jnp.arange(SEL_BLOCK)).reshape(B, KV_GROUPS, Q, n_top * SEL_BLOCK)
    s_s = jnp.einsum('bqghd,bgqkd->bghqk', q, k_sel).astype(f32) * ATTN_SCALE
    p_s = masked_softmax(s_s, (s_pos <= q_pos[None, None, :, None])[:, :, None])
    o_s = jnp.einsum('bghqk,bgqkd->bqghd', p_s.astype(v_sel.dtype), v_sel)
    s_w = jnp.einsum('bqghd,bwgd->bghqw', q, kw).astype(f32) * ATTN_SCALE
    dist = q_pos[:, None] - w_pos[None, :]
    p_w = masked_softmax(s_w, (dist >= 0) & (dist <= WINDOW) & (w_pos[None, :] >= 0))
    o_w = jnp.einsum('bghqw,bwgd->bqghd', p_w.astype(vw.dtype), vw)
    g = gates.astype(o_c.dtype)
    return g[..., 0:1] * o_c + g[..., 1:2] * o_s + g[..., 2:3] * o_w


def nsa_prompt(x, w_in, w_out, cmp):
    B, T, _ = x.shape
    q, kv, gates = nsa_project(x, w_in)
    ctx = nsa_keys(kv[:, :, :4], cmp)
    kw_pad = jnp.pad(kv[:, :, 4:6], ((0, 0), (WINDOW, 0), (0, 0), (0, 0), (0, 0)))
    nb = T // Q_BLOCK
    qb = q.reshape(B, nb, Q_BLOCK, KV_GROUPS, HEADS_PER_GROUP, HEAD_DIM).swapaxes(0, 1)
    gb = gates.reshape(B, nb, Q_BLOCK, KV_GROUPS, HEADS_PER_GROUP, 3).swapaxes(0, 1)

    def block(args):
        qc, gc, c = args
        start = c * Q_BLOCK
        kwc = lax.dynamic_slice_in_dim(kw_pad, start, WINDOW + Q_BLOCK, axis=1)
        q_pos = start + jnp.arange(Q_BLOCK)
        w_pos = start - WINDOW + jnp.arange(WINDOW + Q_BLOCK)
        return nsa_attend(qc, gc, q_pos, ctx, kwc[:, :, 0], kwc[:, :, 1], w_pos)

    o = lax.map(block, (qb, gb, jnp.arange(nb)))
    o = o.swapaxes(0, 1).reshape(B, T, N_HEADS * HEAD_DIM)
    return o @ w_out, kv[:, :, :4], kv[:, T - min(WINDOW, T):, 4:6]


def nsa_sample(x, kv_pool, page_table, win_buf, w_in, w_out, cmp):
    B, S, _ = x.shape
    q, kv, gates = nsa_project(x, w_in)
    past = kv_pool[page_table]
    past = past.reshape(B, -1, 4, KV_GROUPS, HEAD_DIM)
    P = past.shape[1]
    rows = jnp.concatenate([past.astype(kv.dtype), kv[:, :, :4]], axis=1)
    ctx = nsa_keys(rows, cmp)
    Wb = win_buf.shape[1]
    kw = jnp.concatenate([win_buf.astype(kv.dtype), kv[:, :, 4:6]], axis=1)
    q_pos = P + jnp.arange(S)
    w_pos = P - Wb + jnp.arange(Wb + S)
    o = nsa_attend(q, gates, q_pos, ctx, kw[:, :, 0], kw[:, :, 1], w_pos)
    return o.reshape(B, S, N_HEADS * HEAD_DIM) @ w_out, kv[:, :, :4], kw[:, S:]


def rglru_scan(x, h0, ga_w, ga_b, gx_w, gx_b, lam):
    B, T, _ = x.shape
    f32 = jnp.float32
    xb = x.reshape(B, T, RG_BLOCKS, RG_BLOCK_DIM)
    r = jax.nn.sigmoid((jnp.einsum('btnc,ncd->btnd', xb, ga_w).reshape(B, T, D_RNN) + ga_b).astype(f32))
    i = jax.nn.sigmoid((jnp.einsum('btnc,ncd->btnd', xb, gx_w).reshape(B, T, D_RNN) + gx_b).astype(f32))
    log_a = -RG_C * r * jax.nn.softplus(-lam.astype(f32))
    a = jnp.exp(log_a)
    u = jnp.sqrt(-jnp.expm1(2.0 * log_a)) * i * x.astype(f32)

    def step(h, au):
        a_t, u_t = au
        h = a_t * h + u_t
        return h, h

    h_last, hs = lax.scan(step, h0.astype(f32), (a.swapaxes(0, 1), u.swapaxes(0, 1)))
    return hs.swapaxes(0, 1).astype(x.dtype), h_last.astype(h0.dtype)


def rglru_block(x, h0, conv_prev, w_in, conv_w, conv_b, ga_w, ga_b, gx_w, gx_b, lam, w_out):
    u = x @ w_in
    gate_br, rec_br = u[..., :D_RNN], u[..., D_RNN:]
    rec, conv_state = causal_dwconv(rec_br, conv_w, conv_b, conv_prev)
    h, h_last = rglru_scan(rec, h0, ga_w, ga_b, gx_w, gx_b, lam)
    return (h * jax.nn.gelu(gate_br)) @ w_out, h_last, conv_state


def conv_ffn(x, prev, w_up, conv_w, conv_b, w_down):
    u, state = causal_dwconv(x @ w_up, conv_w, conv_b, prev)
    return (jax.nn.gelu(u[..., :D_FF]) * u[..., D_FF:]) @ w_down, state


def layer_tail(x, mix, ffn_prev, p, ln_mg, ln_mb, ln_fg, ln_fb, w_up, conv_w, conv_b, w_down, w_proj, w_gate):
    h = layer_norm(ALPHA * x + mix, ln_mg, ln_mb)
    f, ffn_state = conv_ffn(h, ffn_prev, w_up, conv_w, conv_b, w_down)
    h = layer_norm(ALPHA * h + f, ln_fg, ln_fb)
    return h + jax.nn.sigmoid(h @ w_gate) * (p @ w_proj), ffn_state


def setup_inputs(seed: int = 0) -> dict:
    key = jax.random.key(seed)
    keys = iter(jax.random.split(key, 64))

    def nrm(shape, scale):
        return jax.random.normal(next(keys), shape, jnp.float32) * scale

    n_pages = PAST_LEN // PAGE_SIZE
    n_pool = (DEC_BATCH * n_pages * 5) // 4
    win_buf = min(WINDOW, PAST_LEN)
    A, R = N_ATTN_LAYERS, N_REC_LAYERS
    x_prompt = nrm((BATCH, SEQ, D_MODEL), 1.0)
    x_sample = nrm((DEC_BATCH, DEC_SEQ, D_MODEL), 1.0)
    cache_nsa_kv = nrm((A, n_pool, PAGE_SIZE, 4, KV_GROUPS, HEAD_DIM), 1.0)
    cache_nsa_win = nrm((A, DEC_BATCH, win_buf, 2, KV_GROUPS, HEAD_DIM), 1.0)
    state_rglru_h = nrm((R, DEC_BATCH, D_RNN), 0.5)
    state_rglru_conv = nrm((R, DEC_BATCH, RG_CONV - 1, D_RNN), 1.0)
    state_ffn_conv = nrm((DEPTH, DEC_BATCH, FFN_CONV - 1, 2 * D_FF), 1.0)
    page_table = jax.random.permutation(next(keys), n_pool)[:DEC_BATCH * n_pages]
    page_table = page_table.reshape(DEC_BATCH, n_pages).astype(jnp.int32)
    p_prompt = nrm((DEPTH, BATCH, SEQ, D_PLE), 1.0)
    p_sample = nrm((DEPTH, DEC_BATCH, DEC_SEQ, D_PLE), 1.0)
    u = jax.random.uniform(next(keys), (R, D_RNN), jnp.float32, 0.9, 0.999)
    s = u ** (1.0 / RG_C)
    rg_lambda = jnp.log(s) - jnp.log1p(-s)
    return {
        'x_prompt': x_prompt,
        'x_sample': x_sample,
        'cache_nsa_kv': cache_nsa_kv,
        'cache_nsa_win': cache_nsa_win,
        'state_rglru_h': state_rglru_h,
        'state_rglru_conv': state_rglru_conv,
        'state_ffn_conv': state_ffn_conv,
        'page_table': page_table,
        'p_prompt': p_prompt,
        'p_sample': p_sample,
        'nsa_w_in': nrm((A, D_MODEL, NSA_IN_COLS), D_MODEL ** -0.5),
        'nsa_w_out': nrm((A, N_HEADS * HEAD_DIM, D_MODEL), BETA * (N_HEADS * HEAD_DIM) ** -0.5),
        'nsa_cmp_pe': nrm((A, 2, CMP_BLOCK, HEAD_DIM), 0.1),
        'nsa_cmp_w1': nrm((A, 2, CMP_BLOCK, HEAD_DIM, CMP_HIDDEN), (CMP_BLOCK * HEAD_DIM) ** -0.5),
        'nsa_cmp_b1': nrm((A, 2, CMP_HIDDEN), 0.01),
        'nsa_cmp_w2': nrm((A, 2, CMP_HIDDEN, HEAD_DIM), CMP_HIDDEN ** -0.5),
        'rg_w_in': nrm((R, D_MODEL, 2 * D_RNN), D_MODEL ** -0.5),
        'rg_conv_w': nrm((R, RG_CONV, D_RNN), RG_CONV ** -0.5),
        'rg_conv_b': nrm((R, D_RNN), 0.01),
        'rg_gate_a_w': nrm((R, RG_BLOCKS, RG_BLOCK_DIM, RG_BLOCK_DIM), RG_BLOCK_DIM ** -0.5),
        'rg_gate_a_b': nrm((R, D_RNN), 0.01),
        'rg_gate_x_w': nrm((R, RG_BLOCKS, RG_BLOCK_DIM, RG_BLOCK_DIM), RG_BLOCK_DIM ** -0.5),
        'rg_gate_x_b': nrm((R, D_RNN), 0.01),
        'rg_lambda': rg_lambda,
        'rg_w_out': nrm((R, D_RNN, D_MODEL), BETA * D_RNN ** -0.5),
        'ffn_w_up': nrm((DEPTH, D_MODEL, 2 * D_FF), D_MODEL ** -0.5),
        'ffn_conv_w': nrm((DEPTH, FFN_CONV, 2 * D_FF), FFN_CONV ** -0.5),
        'ffn_conv_b': nrm((DEPTH, 2 * D_FF), 0.01),
        'ffn_w_down': nrm((DEPTH, D_FF, D_MODEL), BETA * D_FF ** -0.5),
        'ln_mix_g': 1.0 + nrm((DEPTH, D_MODEL), 0.05),
        'ln_mix_b': nrm((DEPTH, D_MODEL), 0.02),
        'ln_ffn_g': 1.0 + nrm((DEPTH, D_MODEL), 0.05),
        'ln_ffn_b': nrm((DEPTH, D_MODEL), 0.02),
        'ple_w_proj': nrm((DEPTH, D_PLE, D_MODEL), 0.5 * D_PLE ** -0.5),
        'ple_w_gate': nrm((DEPTH, D_MODEL, D_MODEL), D_MODEL ** -0.5),
    }


def reference(x_prompt, x_sample, cache_nsa_kv, cache_nsa_win, state_rglru_h, state_rglru_conv,
              state_ffn_conv, page_table, p_prompt, p_sample,
              nsa_w_in, nsa_w_out, nsa_cmp_pe, nsa_cmp_w1, nsa_cmp_b1, nsa_cmp_w2,
              rg_w_in, rg_conv_w, rg_conv_b, rg_gate_a_w, rg_gate_a_b, rg_gate_x_w, rg_gate_x_b,
              rg_lambda, rg_w_out,
              ffn_w_up, ffn_conv_w, ffn_conv_b, ffn_w_down,
              ln_mix_g, ln_mix_b, ln_ffn_g, ln_ffn_b, ple_w_proj, ple_w_gate):
    xp, xs = x_prompt, x_sample
    bp = xp.shape[0]
    kv_p, kv_s, win_p, win_s = [], [], [], []
    h_p, h_s, rc_p, rc_s = [], [], [], []
    fc_p, fc_s = [], []
    for i in range(DEPTH):
        j = i // N_MIXERS
        if i % N_MIXERS == 0:
            cmp = (nsa_cmp_pe[j], nsa_cmp_w1[j], nsa_cmp_b1[j], nsa_cmp_w2[j])
            mix_p, rows_p, wnd_p = nsa_prompt(xp, nsa_w_in[j], nsa_w_out[j], cmp)
            mix_s, rows_s, wnd_s = nsa_sample(xs, cache_nsa_kv[j], page_table, cache_nsa_win[j],
                                              nsa_w_in[j], nsa_w_out[j], cmp)
            kv_p.append(rows_p)
            kv_s.append(rows_s)
            win_p.append(wnd_p)
            win_s.append(wnd_s)
        else:
            rg = (rg_w_in[j], rg_conv_w[j], rg_conv_b[j], rg_gate_a_w[j], rg_gate_a_b[j],
                  rg_gate_x_w[j], rg_gate_x_b[j], rg_lambda[j], rg_w_out[j])
            h0 = jnp.zeros((bp, D_RNN), jnp.float32)
            c0 = jnp.zeros((bp, RG_CONV - 1, D_RNN), xp.dtype)
            mix_p, hl_p, cs_p = rglru_block(xp, h0, c0, *rg)
            mix_s, hl_s, cs_s = rglru_block(xs, state_rglru_h[j], state_rglru_conv[j], *rg)
            h_p.append(hl_p)
            h_s.append(hl_s)
            rc_p.append(cs_p)
            rc_s.append(cs_s)
        tail = (ln_mix_g[i], ln_mix_b[i], ln_ffn_g[i], ln_ffn_b[i], ffn_w_up[i], ffn_conv_w[i],
                ffn_conv_b[i], ffn_w_down[i], ple_w_proj[i], ple_w_gate[i])
        f0 = jnp.zeros((bp, FFN_CONV - 1, 2 * D_FF), xp.dtype)
        xp, fs_p = layer_tail(xp, mix_p, f0, p_prompt[i], *tail)
        xs, fs_s = layer_tail(xs, mix_s, state_ffn_conv[i], p_sample[i], *tail)
        fc_p.append(fs_p)
        fc_s.append(fs_s)
    return (xp, xs,
            jnp.stack(kv_p), jnp.stack(kv_s),
            jnp.stack(win_p), jnp.stack(win_s),
            jnp.stack(h_p), jnp.stack(h_s),
            jnp.stack(rc_p), jnp.stack(rc_s),
            jnp.stack(fc_p), jnp.stack(fc_s))
```

```python
import functools

import jax
import jax.numpy as jnp
import numpy as np
from jax import lax
from jax.experimental import pallas as pl
from jax.experimental.pallas import tpu as pltpu

D_MODEL = 1024
DEPTH = 4
N_MIXERS = 2
N_HEADS = 16
HEAD_DIM = D_MODEL // N_HEADS
KV_GROUPS = 2
HEADS_PER_GROUP = N_HEADS // KV_GROUPS
CMP_BLOCK = 32
CMP_STRIDE = 16
SEL_BLOCK = 64
SEL_TOPK = 16
WINDOW = 512
Q_BLOCK = 64
NSA_Q_COLS = N_HEADS * HEAD_DIM
NSA_KV_COLS = 6 * KV_GROUPS * HEAD_DIM
ATTN_SCALE = HEAD_DIM ** -0.5
FORCE_SCORE = 1e6
NEG_INF = -1e30
D_RNN = D_MODEL
RG_BLOCKS = 4
RG_BLOCK_DIM = D_RNN // RG_BLOCKS
RG_CONV = 4
RG_C = 8.0
D_FF = 3 * D_MODEL
FFN_CONV = 3
ALPHA = (2 * DEPTH) ** 0.25
LN_EPS = 1e-5

VMEM_LIMIT_BYTES = 56 * 1024 * 1024
SUBLANES = 8
FF_CHUNK = 512
TAIL_TM = 512


def _const_spec(shape):
    nd = len(shape)
    return pl.BlockSpec(shape, lambda *_: (0,) * nd, pipeline_mode=pl.Buffered(1))


def _layer_norm(x, g, b):
    mu = jnp.mean(x, axis=-1, keepdims=True)
    xc = x - mu
    var = jnp.mean(xc * xc, axis=-1, keepdims=True)
    return xc * lax.rsqrt(var + LN_EPS) * g + b


def _bdot(a, b):
    return jnp.dot(a.astype(jnp.bfloat16), b.astype(jnp.bfloat16),
                   preferred_element_type=jnp.float32)


def _tail_body(x_ref, o_ref, p_ref, halo_ref, wout_ref, wup_ref, wdown_ref, wgate_ref, wproj_ref,
               ln_ref, cw_ref, cb_ref, out_ref, st_ref, tmp_ref, acc_ref, *, halo, shift):
    tm = x_ref.shape[0]
    n_chunks = wdown_ref.shape[0] // FF_CHUNK
    cw = 2 * FF_CHUNK

    @pl.when(pl.program_id(1) == 0)
    def _():
        st_ref[...] = halo_ref[...]

    x = x_ref[...]
    mix = jnp.dot(o_ref[...], wout_ref[...], preferred_element_type=jnp.float32)
    h = _layer_norm(ALPHA * x + mix, ln_ref[0:1, :], ln_ref[1:2, :])
    h_bf = h.astype(jnp.bfloat16)

    for c in range(n_chunks):
        cols = slice(c * cw, (c + 1) * cw)
        u = jnp.dot(h_bf, wup_ref[:, cols], preferred_element_type=jnp.float32)
        tmp_ref[0:halo, :] = st_ref[:, cols]
        tmp_ref[halo:halo + tm, :] = u
        st_ref[:, cols] = u[tm - halo:tm, :]
        p1 = tmp_ref[halo - shift:halo - shift + tm, :]
        p2 = tmp_ref[halo - 2 * shift:halo - 2 * shift + tm, :]
        cv = (cb_ref[:, cols] + p2 * cw_ref[0:1, cols] + p1 * cw_ref[1:2, cols]
              + u * cw_ref[2:3, cols])
        act = jax.nn.gelu(cv[:, :FF_CHUNK]) * cv[:, FF_CHUNK:]
        part = jnp.dot(act.astype(jnp.bfloat16), wdown_ref[c * FF_CHUNK:(c + 1) * FF_CHUNK, :],
                       preferred_element_type=jnp.float32)
        if c == 0:
            acc_ref[...] = part
        else:
            acc_ref[...] += part

    y = _layer_norm(ALPHA * h + acc_ref[...], ln_ref[2:3, :], ln_ref[3:4, :])
    gate = jax.nn.sigmoid(jnp.dot(y.astype(jnp.bfloat16), wgate_ref[...],
                                  preferred_element_type=jnp.float32))
    pp = jnp.dot(p_ref[...].astype(jnp.bfloat16), wproj_ref[...],
                 preferred_element_type=jnp.float32)
    out_ref[...] = y + gate * pp


def _ff_perm(d_ff):
    n_chunks = d_ff // FF_CHUNK
    return np.arange(2 * d_ff).reshape(2, n_chunks, FF_CHUNK).transpose(1, 0, 2).reshape(-1)


def _tail_call(x, o, p, halo0, w, *, n_seq, tm, halo, shift):
    R, D = x.shape
    T = R // n_seq
    nt = T // tm
    F2 = w['up'].shape[1]
    row = lambda b, t: (b * nt + t, 0)
    seq = lambda b, t: (b, 0)
    return pl.pallas_call(
        functools.partial(_tail_body, halo=halo, shift=shift),
        grid=(n_seq, nt),
        in_specs=[pl.BlockSpec((tm, D), row),
                  pl.BlockSpec((tm, o.shape[1]), row),
                  pl.BlockSpec((tm, p.shape[1]), row),
                  pl.BlockSpec((halo, F2), seq),
                  _const_spec(w['out'].shape), _const_spec(w['up'].shape),
                  _const_spec(w['down'].shape), _const_spec(w['gate'].shape),
                  _const_spec(w['proj'].shape), _const_spec(w['ln'].shape),
                  _const_spec(w['cw'].shape), _const_spec(w['cb'].shape)],
        out_specs=[pl.BlockSpec((tm, D), row),
                   pl.BlockSpec((halo, F2), seq)],
        out_shape=[jax.ShapeDtypeStruct((R, D), jnp.float32),
                   jax.ShapeDtypeStruct((n_seq * halo, F2), jnp.float32)],
        scratch_shapes=[pltpu.VMEM((halo + tm, 2 * FF_CHUNK), jnp.float32),
                        pltpu.VMEM((tm, D), jnp.float32)],
        compiler_params=pltpu.CompilerParams(
            dimension_semantics=("arbitrary", "arbitrary"),
            vmem_limit_bytes=VMEM_LIMIT_BYTES),
        name="layer_tail",
    )(x, o, p, halo0, w['out'], w['up'], w['down'], w['gate'], w['proj'], w['ln'], w['cw'], w['cb'])


def _tail_weights(w_out, ln_mg, ln_mb, ln_fg, ln_fb, w_up, conv_w, conv_b, w_down, w_proj, w_gate):
    bf = jnp.bfloat16
    perm = _ff_perm(w_down.shape[0])
    return dict(out=w_out.astype(bf), up=w_up[:, perm].astype(bf), down=w_down.astype(bf),
                gate=w_gate.astype(bf), proj=w_proj.astype(bf),
                ln=jnp.stack([ln_mg, ln_mb, ln_fg, ln_fb]),
                cw=conv_w[:, perm], cb=conv_b[perm][None, :]), perm


def _layer_tail_prompt(xp, op, pp, w, perm):
    B, T, D = xp.shape
    F2 = w['up'].shape[1]
    halo0 = jnp.zeros((B * SUBLANES, F2), jnp.float32)
    y, st = _tail_call(xp.reshape(B * T, D), op.reshape(B * T, -1), pp.reshape(B * T, -1), halo0, w,
                       n_seq=B, tm=min(TAIL_TM, T), halo=SUBLANES, shift=1)
    st = st.reshape(B, SUBLANES, F2)[:, SUBLANES - (FFN_CONV - 1):, :]
    return y.reshape(B, T, D), st[:, :, np.argsort(perm)]


def _layer_tail_sample(xs, os_, ps, state, w, perm):
    S, B, D = xs.shape
    F2 = w['up'].shape[1]
    halo = (FFN_CONV - 1) * B
    halo0 = state[:, :, perm].transpose(1, 0, 2).reshape(halo, F2)
    y, st = _tail_call(xs.reshape(S * B, D), os_.reshape(S * B, -1), ps.reshape(S * B, -1), halo0, w,
                       n_seq=1, tm=S * B, halo=halo, shift=B)
    st = st.reshape(FFN_CONV - 1, B, F2).transpose(1, 0, 2)
    return y.reshape(S, B, D), st[:, :, np.argsort(perm)]


RG_TM = 256


def _rg_body(x_ref, halo_ref, h0_ref, win_ref, cw_ref, cb_ref, gaw_ref, gxw_ref, vec_ref,
             o_ref, st_ref, hl_ref, tmp_ref, *, halo, shift):
    tm = x_ref.shape[0]
    dr = o_ref.shape[1]
    nblk = gaw_ref.shape[0]
    bd = dr // nblk

    @pl.when(pl.program_id(1) == 0)
    def _():
        st_ref[...] = halo_ref[...]
        hl_ref[...] = h0_ref[...]

    u = jnp.dot(x_ref[...].astype(jnp.bfloat16), win_ref[...], preferred_element_type=jnp.float32)
    gate_br = u[:, :dr]
    rec_br = u[:, dr:]
    tmp_ref[0:halo, :] = st_ref[...]
    tmp_ref[halo:halo + tm, :] = rec_br
    st_ref[...] = rec_br[tm - halo:tm, :]
    rec = cb_ref[...] + rec_br * cw_ref[RG_CONV - 1:RG_CONV, :]
    for k in range(1, RG_CONV):
        lo = halo - k * shift
        rec = rec + tmp_ref[lo:lo + tm, :] * cw_ref[RG_CONV - 1 - k:RG_CONV - k, :]

    rec_bf = rec.astype(jnp.bfloat16)
    ga = jnp.concatenate([jnp.dot(rec_bf[:, n * bd:(n + 1) * bd], gaw_ref[n],
                                  preferred_element_type=jnp.float32) for n in range(nblk)], axis=1)
    gx = jnp.concatenate([jnp.dot(rec_bf[:, n * bd:(n + 1) * bd], gxw_ref[n],
                                  preferred_element_type=jnp.float32) for n in range(nblk)], axis=1)
    r = jax.nn.sigmoid(ga + vec_ref[0:1, :])
    i = jax.nn.sigmoid(gx + vec_ref[1:2, :])
    log_a = (-RG_C) * r * jax.nn.softplus(-vec_ref[2:3, :])
    a = jnp.exp(log_a)
    b = jnp.sqrt(-jnp.tanh(log_a) * (a * a + 1.0)) * i * rec

    row = lax.broadcasted_iota(jnp.int32, (tm, 1), 0)
    d = shift
    while d < tm:
        keep = row >= d
        a_sh = jnp.where(keep, pltpu.roll(a, d, 0), 1.0)
        b_sh = jnp.where(keep, pltpu.roll(b, d, 0), 0.0)
        b = b + a * b_sh
        a = a * a_sh
        d *= 2
    carry = hl_ref[0:shift, :]
    hs = b + a * jnp.tile(carry, (tm // shift, 1))
    hl_ref[...] = jnp.tile(hs[tm - shift:tm, :], (hl_ref.shape[0] // shift, 1))
    o_ref[...] = (hs * jax.nn.gelu(gate_br)).astype(o_ref.dtype)


def _rg_call(x, halo0, h0, w, *, n_seq, tm, halo, shift):
    R, D = x.shape
    T = R // n_seq
    nt = T // tm
    dr = w['cw'].shape[1]
    hrows = max(shift, SUBLANES)
    row = lambda b, t: (b * nt + t, 0)
    seq = lambda b, t: (b, 0)
    return pl.pallas_call(
        functools.partial(_rg_body, halo=halo, shift=shift),
        grid=(n_seq, nt),
        in_specs=[pl.BlockSpec((tm, D), row),
                  pl.BlockSpec((halo, dr), seq),
                  pl.BlockSpec((hrows, dr), seq),
                  _const_spec(w['in'].shape), _const_spec(w['cw'].shape), _const_spec(w['cb'].shape),
                  _const_spec(w['ga'].shape), _const_spec(w['gx'].shape), _const_spec(w['vec'].shape)],
        out_specs=[pl.BlockSpec((tm, dr), row),
                   pl.BlockSpec((halo, dr), seq),
                   pl.BlockSpec((hrows, dr), seq)],
        out_shape=[jax.ShapeDtypeStruct((R, dr), jnp.bfloat16),
                   jax.ShapeDtypeStruct((n_seq * halo, dr), jnp.float32),
                   jax.ShapeDtypeStruct((n_seq * hrows, dr), jnp.float32)],
        scratch_shapes=[pltpu.VMEM((halo + tm, dr), jnp.float32)],
        compiler_params=pltpu.CompilerParams(
            dimension_semantics=("arbitrary", "arbitrary"),
            vmem_limit_bytes=VMEM_LIMIT_BYTES),
        name="rglru",
    )(x, halo0, h0, w['in'], w['cw'], w['cb'], w['ga'], w['gx'], w['vec'])


def _rg_weights(w_in, conv_w, conv_b, ga_w, ga_b, gx_w, gx_b, lam):
    bf = jnp.bfloat16
    return dict(**{'in': w_in.astype(bf)}, cw=conv_w, cb=conv_b[None, :], ga=ga_w.astype(bf),
                gx=gx_w.astype(bf), vec=jnp.stack([ga_b, gx_b, lam]))


def _rglru_prompt(xp, w):
    B, T, D = xp.shape
    dr = w['cw'].shape[1]
    o, st, hl = _rg_call(xp.reshape(B * T, D), jnp.zeros((B * SUBLANES, dr), jnp.float32),
                         jnp.zeros((B * SUBLANES, dr), jnp.float32), w,
                         n_seq=B, tm=min(RG_TM, T), halo=SUBLANES, shift=1)
    st = st.reshape(B, SUBLANES, dr)[:, SUBLANES - (RG_CONV - 1):, :]
    return o.reshape(B, T, dr), hl.reshape(B, SUBLANES, dr)[:, 0, :], st


def _rglru_sample(xs, h0, conv_prev, w):
    S, B, D = xs.shape
    dr = w['cw'].shape[1]
    halo = (RG_CONV - 1) * B
    o, st, hl = _rg_call(xs.reshape(S * B, D), conv_prev.swapaxes(0, 1).reshape(halo, dr), h0, w,
                         n_seq=1, tm=S * B, halo=halo, shift=B)
    return o.reshape(S, B, dr), hl, st.reshape(RG_CONV - 1, B, dr).swapaxes(0, 1)


def _mm_body(x_ref, w_ref, o_ref):
    o_ref[...] = jnp.dot(x_ref[...].astype(jnp.bfloat16), w_ref[...],
                         preferred_element_type=jnp.float32)


def _matmul(x, w, tm=512, tn=512):
    M, K = x.shape
    N = w.shape[1]
    tm = min(tm, M)
    tn = min(tn, N)
    assert M % tm == 0 and N % tn == 0, (M, N, tm, tn)
    return pl.pallas_call(
        _mm_body,
        grid=(M // tm, N // tn),
        in_specs=[pl.BlockSpec((tm, K), lambda i, j: (i, 0)),
                  pl.BlockSpec((K, tn), lambda i, j: (0, j))],
        out_specs=pl.BlockSpec((tm, tn), lambda i, j: (i, j)),
        out_shape=jax.ShapeDtypeStruct((M, N), jnp.float32),
        compiler_params=pltpu.CompilerParams(
            dimension_semantics=("arbitrary", "arbitrary"),
            vmem_limit_bytes=VMEM_LIMIT_BYTES),
    )(x, w.astype(jnp.bfloat16))


def _mm3(x, w):
    B, T, K = x.shape
    N = w.shape[1]
    Np = -(-N // 128) * 128
    if Np != N:
        w = jnp.pad(w, ((0, 0), (0, Np - N)))
    tn = 512 if Np % 512 == 0 else (256 if Np % 256 == 0 else 128)
    y = _matmul(x.reshape(B * T, K), w, tn=tn)
    return y[:, :N].reshape(B, T, N)


def _masked_softmax(s, mask):
    s = jnp.where(mask, s, NEG_INF)
    m = jnp.max(s, axis=-1, keepdims=True)
    e = jnp.where(mask, jnp.exp(s - m), 0.0)
    return e / jnp.maximum(jnp.sum(e, axis=-1, keepdims=True), 1e-30)


def _causal_dwconv(u, w, b, prev):
    K = w.shape[0]
    T = u.shape[1]
    ext = jnp.concatenate([prev.astype(u.dtype), u], axis=1)
    out = b + ext[:, 0:T] * w[0]
    for k in range(1, K):
        out = out + ext[:, k:k + T] * w[k]
    return out, ext[:, T:]


def _nsa_project(x, w_in):
    B, T, _ = x.shape
    proj = _mm3(x, w_in)
    q = proj[..., :NSA_Q_COLS].reshape(B, T, KV_GROUPS, HEADS_PER_GROUP, HEAD_DIM)
    kv = proj[..., NSA_Q_COLS:NSA_Q_COLS + NSA_KV_COLS].reshape(B, T, 6, KV_GROUPS, HEAD_DIM)
    gates = jax.nn.sigmoid(proj[..., NSA_Q_COLS + NSA_KV_COLS:]).reshape(B, T, KV_GROUPS, HEADS_PER_GROUP, 3)
    return q, kv, gates


def _compress(rows, pe, w1, b1, w2):
    nc = (rows.shape[1] - CMP_BLOCK) // CMP_STRIDE + 1
    idx = jnp.arange(nc)[:, None] * CMP_STRIDE + jnp.arange(CMP_BLOCK)[None, :]
    blk = rows[:, idx] + pe[:, None, :]
    hid = jax.nn.gelu(jnp.einsum('bclgd,ldh->bcgh', blk, w1) + b1)
    return jnp.einsum('bcgh,hd->bcgd', hid, w2)


def _nsa_keys(rows, cmp):
    pe, w1, b1, w2 = cmp
    B, Tk = rows.shape[:2]
    kc = _compress(rows[:, :, 0], pe[0], w1[0], b1[0], w2[0])
    vc = _compress(rows[:, :, 1], pe[1], w1[1], b1[1], w2[1])
    nc = kc.shape[1]
    c_start = jnp.arange(nc) * CMP_STRIDE
    c_end = c_start + (CMP_BLOCK - 1)
    ns = -(-Tk // SEL_BLOCK)
    b_start = jnp.arange(ns) * SEL_BLOCK
    overlap = ((c_start[:, None] < b_start[None, :] + SEL_BLOCK)
               & (c_end[:, None] >= b_start[None, :])).astype(jnp.float32)
    sel = jnp.pad(rows[:, :, 2:4], ((0, 0), (0, ns * SEL_BLOCK - Tk), (0, 0), (0, 0), (0, 0)))
    sel = sel.reshape(B, ns, SEL_BLOCK, 2, KV_GROUPS, HEAD_DIM).transpose(3, 0, 4, 1, 2, 5)
    return (kc, vc, c_end, sel[0], sel[1], overlap)


def _nsa_attend(q, gates, q_pos, ctx, kw, vw, w_pos):
    kc, vc, c_end, ks, vs, overlap = ctx
    B, Q = q.shape[:2]
    ns = ks.shape[2]
    f32 = jnp.float32
    s_c = jnp.einsum('bqghd,bcgd->bghqc', q, kc).astype(f32) * ATTN_SCALE
    p_c = _masked_softmax(s_c, c_end[None, :] <= q_pos[:, None])
    o_c = jnp.einsum('bghqc,bcgd->bqghd', p_c, vc)
    imp = jnp.einsum('bghqc,cn->bgqn', p_c, overlap)
    blk = jnp.arange(ns)[None, :]
    cur = (q_pos // SEL_BLOCK)[:, None]
    imp = jnp.where((blk == 0) | (blk == cur) | (blk == cur - 1), FORCE_SCORE, imp)
    imp = jnp.where(blk > cur, -1.0, imp)
    n_top = min(SEL_TOPK, ns)
    _, idx = lax.top_k(imp, n_top)
    bi = jnp.arange(B)[:, None, None, None]
    gi = jnp.arange(KV_GROUPS)[None, :, None, None]
    k_sel = ks[bi, gi, idx].reshape(B, KV_GROUPS, Q, n_top * SEL_BLOCK, HEAD_DIM)
    v_sel = vs[bi, gi, idx].reshape(B, KV_GROUPS, Q, n_top * SEL_BLOCK, HEAD_DIM)
    s_pos = (idx[..., None] * SEL_BLOCK + jnp.arange(SEL_BLOCK)).reshape(B, KV_GROUPS, Q, n_top * SEL_BLOCK)
    s_s = jnp.einsum('bqghd,bgqkd->bghqk', q, k_sel).astype(f32) * ATTN_SCALE
    p_s = _masked_softmax(s_s, (s_pos <= q_pos[None, None, :, None])[:, :, None])
    o_s = jnp.einsum('bghqk,bgqkd->bqghd', p_s, v_sel)
    s_w = jnp.einsum('bqghd,bwgd->bghqw', q, kw).astype(f32) * ATTN_SCALE
    dist = q_pos[:, None] - w_pos[None, :]
    p_w = _masked_softmax(s_w, (dist >= 0) & (dist <= WINDOW) & (w_pos[None, :] >= 0))
    o_w = jnp.einsum('bghqw,bwgd->bqghd', p_w, vw)
    g = gates
    return g[..., 0:1] * o_c + g[..., 1:2] * o_s + g[..., 2:3] * o_w


def _nsa_prompt(x, w_in, cmp):
    B, T, _ = x.shape
    q, kv, gates = _nsa_project(x, w_in)
    ctx = _nsa_keys(kv[:, :, :4], cmp)
    kw_pad = jnp.pad(kv[:, :, 4:6], ((0, 0), (WINDOW, 0), (0, 0), (0, 0), (0, 0)))
    nb = T // Q_BLOCK
    qb = q.reshape(B, nb, Q_BLOCK, KV_GROUPS, HEADS_PER_GROUP, HEAD_DIM).swapaxes(0, 1)
    gb = gates.reshape(B, nb, Q_BLOCK, KV_GROUPS, HEADS_PER_GROUP, 3).swapaxes(0, 1)

    def block(args):
        qc, gc, c = args
        start = c * Q_BLOCK
        kwc = lax.dynamic_slice_in_dim(kw_pad, start, WINDOW + Q_BLOCK, axis=1)
        q_pos = start + jnp.arange(Q_BLOCK)
        w_pos = start - WINDOW + jnp.arange(WINDOW + Q_BLOCK)
        return _nsa_attend(qc, gc, q_pos, ctx, kwc[:, :, 0], kwc[:, :, 1], w_pos)

    o = lax.map(block, (qb, gb, jnp.arange(nb)))
    o = o.swapaxes(0, 1).reshape(B, T, N_HEADS * HEAD_DIM)
    return o, kv[:, :, :4], kv[:, T - min(WINDOW, T):, 4:6]


def _nsa_sample(x, kv_pool, page_table, win_buf, w_in, cmp):
    B, S, _ = x.shape
    q, kv, gates = _nsa_project(x, w_in)
    past = kv_pool[page_table]
    past = past.reshape(B, -1, 4, KV_GROUPS, HEAD_DIM)
    P = past.shape[1]
    rows = jnp.concatenate([past, kv[:, :, :4]], axis=1)
    ctx = _nsa_keys(rows, cmp)
    Wb = win_buf.shape[1]
    kw = jnp.concatenate([win_buf, kv[:, :, 4:6]], axis=1)
    q_pos = P + jnp.arange(S)
    w_pos = P - Wb + jnp.arange(Wb + S)
    o = _nsa_attend(q, gates, q_pos, ctx, kw[:, :, 0], kw[:, :, 1], w_pos)
    return o.reshape(B, S, N_HEADS * HEAD_DIM), kv[:, :, :4], kw[:, S:]


def kernel(x_prompt, x_sample, cache_nsa_kv, cache_nsa_win, state_rglru_h, state_rglru_conv,
           state_ffn_conv, page_table, p_prompt, p_sample,
           nsa_w_in, nsa_w_out, nsa_cmp_pe, nsa_cmp_w1, nsa_cmp_b1, nsa_cmp_w2,
           rg_w_in, rg_conv_w, rg_conv_b, rg_gate_a_w, rg_gate_a_b, rg_gate_x_w, rg_gate_x_b,
           rg_lambda, rg_w_out,
           ffn_w_up, ffn_conv_w, ffn_conv_b, ffn_w_down,
           ln_mix_g, ln_mix_b, ln_ffn_g, ln_ffn_b, ple_w_proj, ple_w_gate):
    xp = x_prompt
    xs = x_sample.swapaxes(0, 1)
    bf = jnp.bfloat16
    kv_p, kv_s, win_p, win_s = [], [], [], []
    h_p, h_s, rc_p, rc_s = [], [], [], []
    fc_p, fc_s = [], []
    for i in range(DEPTH):
        j = i // N_MIXERS
        if i % N_MIXERS == 0:
            cmp = (nsa_cmp_pe[j], nsa_cmp_w1[j], nsa_cmp_b1[j], nsa_cmp_w2[j])
            o_p, rows_p, wnd_p = _nsa_prompt(xp, nsa_w_in[j], cmp)
            o_s, rows_s, wnd_s = _nsa_sample(xs.swapaxes(0, 1), cache_nsa_kv[j], page_table,
                                             cache_nsa_win[j], nsa_w_in[j], cmp)
            o_p = o_p.astype(bf)
            o_s = o_s.astype(bf).swapaxes(0, 1)
            w_out = nsa_w_out[j]
            kv_p.append(rows_p)
            kv_s.append(rows_s)
            win_p.append(wnd_p)
            win_s.append(wnd_s)
        else:
            rgw = _rg_weights(rg_w_in[j], rg_conv_w[j], rg_conv_b[j], rg_gate_a_w[j], rg_gate_a_b[j],
                              rg_gate_x_w[j], rg_gate_x_b[j], rg_lambda[j])
            o_p, hl_p, cs_p = _rglru_prompt(xp, rgw)
            o_s, hl_s, cs_s = _rglru_sample(xs, state_rglru_h[j], state_rglru_conv[j], rgw)
            w_out = rg_w_out[j]
            h_p.append(hl_p)
            h_s.append(hl_s)
            rc_p.append(cs_p)
            rc_s.append(cs_s)
        w, perm = _tail_weights(w_out, ln_mix_g[i], ln_mix_b[i], ln_ffn_g[i], ln_ffn_b[i], ffn_w_up[i],
                                ffn_conv_w[i], ffn_conv_b[i], ffn_w_down[i], ple_w_proj[i], ple_w_gate[i])
        xp, fs_p = _layer_tail_prompt(xp, o_p, p_prompt[i], w, perm)
        xs, fs_s = _layer_tail_sample(xs, o_s, p_sample[i].swapaxes(0, 1), state_ffn_conv[i], w, perm)
        fc_p.append(fs_p)
        fc_s.append(fs_s)
    return (xp, xs.swapaxes(0, 1),
            jnp.stack(kv_p), jnp.stack(kv_s),
            jnp.stack(win_p), jnp.stack(win_s),
            jnp.stack(h_p), jnp.stack(h_s),
            jnp.stack(rc_p), jnp.stack(rc_s),
            jnp.stack(fc_p), jnp.stack(fc_s))
```

```python
import functools

import jax
import jax.numpy as jnp
import numpy as np
from jax import lax
from jax.experimental import pallas as pl
from jax.experimental.pallas import tpu as pltpu

D_MODEL = 1024
DEPTH = 4
N_MIXERS = 2
N_HEADS = 16
HEAD_DIM = D_MODEL // N_HEADS
KV_GROUPS = 2
HEADS_PER_GROUP = N_HEADS // KV_GROUPS
CMP_BLOCK = 32
CMP_STRIDE = 16
SEL_BLOCK = 64
SEL_TOPK = 16
WINDOW = 512
Q_BLOCK = 64
NSA_Q_COLS = N_HEADS * HEAD_DIM
NSA_KV_COLS = 6 * KV_GROUPS * HEAD_DIM
ATTN_SCALE = HEAD_DIM ** -0.5
FORCE_SCORE = 1e6
NEG_INF = -1e30
D_RNN = D_MODEL
RG_BLOCKS = 4
RG_BLOCK_DIM = D_RNN // RG_BLOCKS
RG_CONV = 4
RG_C = 8.0
D_FF = 3 * D_MODEL
FFN_CONV = 3
ALPHA = (2 * DEPTH) ** 0.25
LN_EPS = 1e-5

VMEM_LIMIT_BYTES = 56 * 1024 * 1024
SUBLANES = 8
FF_CHUNK = 512
TAIL_TM = 512


def _const_spec(shape):
    nd = len(shape)
    return pl.BlockSpec(shape, lambda *_: (0,) * nd, pipeline_mode=pl.Buffered(1))


def _layer_norm(x, g, b):
    mu = jnp.mean(x, axis=-1, keepdims=True)
    xc = x - mu
    var = jnp.mean(xc * xc, axis=-1, keepdims=True)
    return xc * lax.rsqrt(var + LN_EPS) * g + b


def _tail_body(x_ref, o_ref, p_ref, halo_ref, wout_ref, wup_ref, wdown_ref, wgate_ref, wproj_ref,
               ln_ref, cw_ref, cb_ref, out_ref, st_ref, tmp_ref, acc_ref, *, halo, shift):
    tm = x_ref.shape[0]
    n_chunks = wdown_ref.shape[0] // FF_CHUNK
    cw = 2 * FF_CHUNK

    @pl.when(pl.program_id(1) == 0)
    def _():
        st_ref[...] = halo_ref[...]

    x = x_ref[...]
    mix = jnp.dot(o_ref[...], wout_ref[...], preferred_element_type=jnp.float32)
    h = _layer_norm(ALPHA * x + mix, ln_ref[0:1, :], ln_ref[1:2, :])
    h_bf = h.astype(jnp.bfloat16)

    for c in range(n_chunks):
        cols = slice(c * cw, (c + 1) * cw)
        u = jnp.dot(h_bf, wup_ref[:, cols], preferred_element_type=jnp.float32)
        tmp_ref[0:halo, :] = st_ref[:, cols]
        tmp_ref[halo:halo + tm, :] = u
        st_ref[:, cols] = u[tm - halo:tm, :]
        p1 = tmp_ref[halo - shift:halo - shift + tm, :]
        p2 = tmp_ref[halo - 2 * shift:halo - 2 * shift + tm, :]
        cv = (cb_ref[:, cols] + p2 * cw_ref[0:1, cols] + p1 * cw_ref[1:2, cols]
              + u * cw_ref[2:3, cols])
        act = jax.nn.gelu(cv[:, :FF_CHUNK]) * cv[:, FF_CHUNK:]
        part = jnp.dot(act.astype(jnp.bfloat16), wdown_ref[c * FF_CHUNK:(c + 1) * FF_CHUNK, :],
                       preferred_element_type=jnp.float32)
        if c == 0:
            acc_ref[...] = part
        else:
            acc_ref[...] += part

    y = _layer_norm(ALPHA * h + acc_ref[...], ln_ref[2:3, :], ln_ref[3:4, :])
    gate = jax.nn.sigmoid(jnp.dot(y.astype(jnp.bfloat16), wgate_ref[...],
                                  preferred_element_type=jnp.float32))
    pp = jnp.dot(p_ref[...].astype(jnp.bfloat16), wproj_ref[...],
                 preferred_element_type=jnp.float32)
    out_ref[...] = y + gate * pp


def _ff_permute(a):
    lead = a.shape[:-1]
    return a.reshape(*lead, 2, -1, FF_CHUNK).swapaxes(-3, -2).reshape(*lead, -1)


def _ff_unpermute(a):
    lead = a.shape[:-1]
    return a.reshape(*lead, -1, 2, FF_CHUNK).swapaxes(-3, -2).reshape(*lead, -1)


def _tail_call(x, o, p, halo0, w, *, n_seq, tm, halo, shift):
    R, D = x.shape
    T = R // n_seq
    nt = T // tm
    F2 = w['up'].shape[1]
    row = lambda b, t: (b * nt + t, 0)
    seq = lambda b, t: (b, 0)
    return pl.pallas_call(
        functools.partial(_tail_body, halo=halo, shift=shift),
        grid=(n_seq, nt),
        in_specs=[pl.BlockSpec((tm, D), row),
                  pl.BlockSpec((tm, o.shape[1]), row),
                  pl.BlockSpec((tm, p.shape[1]), row),
                  pl.BlockSpec((halo, F2), seq),
                  _const_spec(w['out'].shape), _const_spec(w['up'].shape),
                  _const_spec(w['down'].shape), _const_spec(w['gate'].shape),
                  _const_spec(w['proj'].shape), _const_spec(w['ln'].shape),
                  _const_spec(w['cw'].shape), _const_spec(w['cb'].shape)],
        out_specs=[pl.BlockSpec((tm, D), row),
                   pl.BlockSpec((halo, F2), seq)],
        out_shape=[jax.ShapeDtypeStruct((R, D), jnp.float32),
                   jax.ShapeDtypeStruct((n_seq * halo, F2), jnp.float32)],
        scratch_shapes=[pltpu.VMEM((halo + tm, 2 * FF_CHUNK), jnp.float32),
                        pltpu.VMEM((tm, D), jnp.float32)],
        compiler_params=pltpu.CompilerParams(
            dimension_semantics=("arbitrary", "arbitrary"),
            vmem_limit_bytes=VMEM_LIMIT_BYTES),
        name="layer_tail",
    )(x, o, p, halo0, w['out'], w['up'], w['down'], w['gate'], w['proj'], w['ln'], w['cw'], w['cb'])


def _tail_weights(w_out, ln_mg, ln_mb, ln_fg, ln_fb, w_up, conv_w, conv_b, w_down, w_proj, w_gate):
    bf = jnp.bfloat16
    return dict(out=w_out.astype(bf), up=_ff_permute(w_up).astype(bf), down=w_down.astype(bf),
                gate=w_gate.astype(bf), proj=w_proj.astype(bf),
                ln=jnp.stack([ln_mg, ln_mb, ln_fg, ln_fb]),
                cw=_ff_permute(conv_w), cb=_ff_permute(conv_b)[None, :])


def _layer_tail_prompt(xp, op, pp, w):
    B, T, D = xp.shape
    F2 = w['up'].shape[1]
    halo0 = jnp.zeros((B * SUBLANES, F2), jnp.float32)
    y, st = _tail_call(xp.reshape(B * T, D), op.reshape(B * T, -1), pp.reshape(B * T, -1), halo0, w,
                       n_seq=B, tm=min(TAIL_TM, T), halo=SUBLANES, shift=1)
    st = st.reshape(B, SUBLANES, F2)[:, SUBLANES - (FFN_CONV - 1):, :]
    return y.reshape(B, T, D), _ff_unpermute(st)


def _layer_tail_sample(xs, os_, ps, state, w):
    S, B, D = xs.shape
    F2 = w['up'].shape[1]
    halo = (FFN_CONV - 1) * B
    halo0 = _ff_permute(state).transpose(1, 0, 2).reshape(halo, F2)
    y, st = _tail_call(xs.reshape(S * B, D), os_.reshape(S * B, -1), ps.reshape(S * B, -1), halo0, w,
                       n_seq=1, tm=S * B, halo=halo, shift=B)
    st = st.reshape(FFN_CONV - 1, B, F2).transpose(1, 0, 2)
    return y.reshape(S, B, D), _ff_unpermute(st)


RG_TM = 256


def _rg_body(x_ref, halo_ref, h0_ref, win_ref, cw_ref, cb_ref, gaw_ref, gxw_ref, vec_ref,
             o_ref, st_ref, hl_ref, tmp_ref, *, halo, shift):
    tm = x_ref.shape[0]
    dr = o_ref.shape[1]
    nblk = gaw_ref.shape[0]
    bd = dr // nblk

    @pl.when(pl.program_id(1) == 0)
    def _():
        st_ref[...] = halo_ref[...]
        hl_ref[...] = h0_ref[...]

    u = jnp.dot(x_ref[...].astype(jnp.bfloat16), win_ref[...], preferred_element_type=jnp.float32)
    gate_br = u[:, :dr]
    rec_br = u[:, dr:]
    tmp_ref[0:halo, :] = st_ref[...]
    tmp_ref[halo:halo + tm, :] = rec_br
    st_ref[...] = rec_br[tm - halo:tm, :]
    rec = cb_ref[...] + rec_br * cw_ref[RG_CONV - 1:RG_CONV, :]
    for k in range(1, RG_CONV):
        lo = halo - k * shift
        rec = rec + tmp_ref[lo:lo + tm, :] * cw_ref[RG_CONV - 1 - k:RG_CONV - k, :]

    rec_bf = rec.astype(jnp.bfloat16)
    ga = jnp.concatenate([jnp.dot(rec_bf[:, n * bd:(n + 1) * bd], gaw_ref[n],
                                  preferred_element_type=jnp.float32) for n in range(nblk)], axis=1)
    gx = jnp.concatenate([jnp.dot(rec_bf[:, n * bd:(n + 1) * bd], gxw_ref[n],
                                  preferred_element_type=jnp.float32) for n in range(nblk)], axis=1)
    r = jax.nn.sigmoid(ga + vec_ref[0:1, :])
    i = jax.nn.sigmoid(gx + vec_ref[1:2, :])
    log_a = (-RG_C) * r * jax.nn.softplus(-vec_ref[2:3, :])
    a = jnp.exp(log_a)
    b = jnp.sqrt(-jnp.tanh(log_a) * (a * a + 1.0)) * i * rec

    row = lax.broadcasted_iota(jnp.int32, (tm, 1), 0)
    d = shift
    while d < tm:
        keep = row >= d
        a_sh = jnp.where(keep, pltpu.roll(a, d, 0), 1.0)
        b_sh = jnp.where(keep, pltpu.roll(b, d, 0), 0.0)
        b = b + a * b_sh
        a = a * a_sh
        d *= 2
    carry = hl_ref[0:shift, :]
    hs = b + a * jnp.tile(carry, (tm // shift, 1))
    hl_ref[...] = jnp.tile(hs[tm - shift:tm, :], (hl_ref.shape[0] // shift, 1))
    o_ref[...] = (hs * jax.nn.gelu(gate_br)).astype(o_ref.dtype)


def _rg_call(x, halo0, h0, w, *, n_seq, tm, halo, shift):
    R, D = x.shape
    T = R // n_seq
    nt = T // tm
    dr = w['cw'].shape[1]
    hrows = max(shift, SUBLANES)
    row = lambda b, t: (b * nt + t, 0)
    seq = lambda b, t: (b, 0)
    return pl.pallas_call(
        functools.partial(_rg_body, halo=halo, shift=shift),
        grid=(n_seq, nt),
        in_specs=[pl.BlockSpec((tm, D), row),
                  pl.BlockSpec((halo, dr), seq),
                  pl.BlockSpec((hrows, dr), seq),
                  _const_spec(w['in'].shape), _const_spec(w['cw'].shape), _const_spec(w['cb'].shape),
                  _const_spec(w['ga'].shape), _const_spec(w['gx'].shape), _const_spec(w['vec'].shape)],
        out_specs=[pl.BlockSpec((tm, dr), row),
                   pl.BlockSpec((halo, dr), seq),
                   pl.BlockSpec((hrows, dr), seq)],
        out_shape=[jax.ShapeDtypeStruct((R, dr), jnp.bfloat16),
                   jax.ShapeDtypeStruct((n_seq * halo, dr), jnp.float32),
                   jax.ShapeDtypeStruct((n_seq * hrows, dr), jnp.float32)],
        scratch_shapes=[pltpu.VMEM((halo + tm, dr), jnp.float32)],
        compiler_params=pltpu.CompilerParams(
            dimension_semantics=("arbitrary", "arbitrary"),
            vmem_limit_bytes=VMEM_LIMIT_BYTES),
        name="rglru",
    )(x, halo0, h0, w['in'], w['cw'], w['cb'], w['ga'], w['gx'], w['vec'])


def _rg_weights(w_in, conv_w, conv_b, ga_w, ga_b, gx_w, gx_b, lam):
    bf = jnp.bfloat16
    return dict(**{'in': w_in.astype(bf)}, cw=conv_w, cb=conv_b[None, :], ga=ga_w.astype(bf),
                gx=gx_w.astype(bf), vec=jnp.stack([ga_b, gx_b, lam]))


def _rglru_prompt(xp, w):
    B, T, D = xp.shape
    dr = w['cw'].shape[1]
    o, st, hl = _rg_call(xp.reshape(B * T, D), jnp.zeros((B * SUBLANES, dr), jnp.float32),
                         jnp.zeros((B * SUBLANES, dr), jnp.float32), w,
                         n_seq=B, tm=min(RG_TM, T), halo=SUBLANES, shift=1)
    st = st.reshape(B, SUBLANES, dr)[:, SUBLANES - (RG_CONV - 1):, :]
    return o.reshape(B, T, dr), hl.reshape(B, SUBLANES, dr)[:, 0, :], st


def _rglru_sample(xs, h0, conv_prev, w):
    S, B, D = xs.shape
    dr = w['cw'].shape[1]
    halo = (RG_CONV - 1) * B
    o, st, hl = _rg_call(xs.reshape(S * B, D), conv_prev.swapaxes(0, 1).reshape(halo, dr), h0, w,
                         n_seq=1, tm=S * B, halo=halo, shift=B)
    return o.reshape(S, B, dr), hl, st.reshape(RG_CONV - 1, B, dr).swapaxes(0, 1)


LANES = 128
NSA_PROJ_TM = 512
N_KV_KINDS = 6
GATE_COLS = 3 * N_HEADS


def _nsa_proj_body(x_ref, w_ref, q_ref, kv_ref, g_ref, *rest, head_major):
    proj = jnp.dot(x_ref[...].astype(jnp.bfloat16), w_ref[...], preferred_element_type=jnp.float32)
    q = proj[:, :NSA_Q_COLS] * ATTN_SCALE
    kv = proj[:, NSA_Q_COLS:NSA_Q_COLS + NSA_KV_COLS]
    q_ref[...] = q.astype(q_ref.dtype)
    kv_ref[...] = kv
    g_ref[...] = jax.nn.sigmoid(proj[:, NSA_Q_COLS + NSA_KV_COLS:])
    if head_major:
        qh_ref, kvh_ref = rest
        for h in range(N_HEADS):
            qh_ref[0, h] = q[:, h * HEAD_DIM:(h + 1) * HEAD_DIM].astype(qh_ref.dtype)
        for k in range(N_KV_KINDS * KV_GROUPS):
            kvh_ref[0, k] = kv[:, k * HEAD_DIM:(k + 1) * HEAD_DIM].astype(kvh_ref.dtype)


def _nsa_proj(x, w_in, *, n_seq, head_major):
    R, D = x.shape
    T = R // n_seq
    tm = min(NSA_PROJ_TM, T)
    nt = T // tm
    n_in = w_in.shape[1]
    n_pad = NSA_Q_COLS + NSA_KV_COLS + LANES
    w = jnp.pad(w_in, ((0, 0), (0, n_pad - n_in))).astype(jnp.bfloat16)
    row = lambda b, t: (b * nt + t, 0)
    out_specs = [pl.BlockSpec((tm, NSA_Q_COLS), row), pl.BlockSpec((tm, NSA_KV_COLS), row),
                 pl.BlockSpec((tm, LANES), row)]
    out_shape = [jax.ShapeDtypeStruct((R, NSA_Q_COLS), jnp.bfloat16),
                 jax.ShapeDtypeStruct((R, NSA_KV_COLS), jnp.float32),
                 jax.ShapeDtypeStruct((R, LANES), jnp.float32)]
    if head_major:
        hm = lambda b, t: (b, 0, t, 0)
        out_specs += [pl.BlockSpec((1, N_HEADS, tm, HEAD_DIM), hm),
                      pl.BlockSpec((1, N_KV_KINDS * KV_GROUPS, tm, HEAD_DIM), hm)]
        out_shape += [jax.ShapeDtypeStruct((n_seq, N_HEADS, T, HEAD_DIM), jnp.bfloat16),
                      jax.ShapeDtypeStruct((n_seq, N_KV_KINDS * KV_GROUPS, T, HEAD_DIM), jnp.bfloat16)]
    return pl.pallas_call(
        functools.partial(_nsa_proj_body, head_major=head_major),
        grid=(n_seq, nt),
        in_specs=[pl.BlockSpec((tm, D), row), _const_spec(w.shape)],
        out_specs=out_specs,
        out_shape=out_shape,
        compiler_params=pltpu.CompilerParams(
            dimension_semantics=("arbitrary", "arbitrary"),
            vmem_limit_bytes=VMEM_LIMIT_BYTES),
        name="nsa_proj",
    )(x, w)


N_CHUNKS = 128
CMP_HALF = CMP_BLOCK // CMP_STRIDE


def _compress_rows(rows_ref, wp_ref, cb, w2_ref):
    n = N_CHUNKS
    hid = w2_ref.shape[0]
    lane = lax.broadcasted_iota(jnp.int32, (n, LANES), 1)
    low = lane < HEAD_DIM
    z = jnp.zeros((KV_GROUPS * n, 2 * hid), jnp.float32)
    for j in range(CMP_STRIDE // 2):
        va = rows_ref[pl.ds(2 * j, n, stride=CMP_STRIDE), :]
        vb = rows_ref[pl.ds(2 * j + 1, n, stride=CMP_STRIDE), :]
        g0 = jnp.where(low, va, pltpu.roll(vb, HEAD_DIM, 1))
        g1 = jnp.where(low, pltpu.roll(va, HEAD_DIM, 1), vb)
        lhs = jnp.concatenate([g0, g1], axis=0).astype(jnp.bfloat16)
        z = z + jnp.dot(lhs, wp_ref[j], preferred_element_type=jnp.float32)
    za = z[:, :hid]
    zb = pltpu.roll(z[:, hid:], KV_GROUPS * n - 1, 0)
    h = jax.nn.gelu(za + zb + cb)
    return jnp.dot(h.astype(jnp.bfloat16), w2_ref[...], preferred_element_type=jnp.float32)


def _cmp_bias(pe_ref, w1_ref, b1_ref):
    return b1_ref[...] + jnp.sum(pe_ref[...] * w1_ref[...], axis=0, keepdims=True)


def _cmp_weights(pe, w1, b1, w2):
    kinds, L, hd, hid = w1.shape
    half = L // 2
    wa = w1[:, :half].reshape(kinds, half // 2, 2 * hd, hid)
    wb = w1[:, half:].reshape(kinds, half // 2, 2 * hd, hid)
    return dict(wp=jnp.concatenate([wa, wb], axis=-1).astype(jnp.bfloat16),
                pe=pe.reshape(kinds, L * hd, 1), w1=w1.reshape(kinds, L * hd, hid),
                b1=b1.reshape(kinds, 1, hid), w2=w2.astype(jnp.bfloat16))


def _cmp_prompt_body(krows_ref, vrows_ref, wp_ref, pe_ref, w1_ref, b1_ref, w2_ref, out_ref):
    for kind, rows_ref in enumerate((krows_ref, vrows_ref)):
        cb = _cmp_bias(pe_ref.at[kind], w1_ref.at[kind], b1_ref.at[kind])
        res = _compress_rows(rows_ref, wp_ref.at[kind], cb, w2_ref.at[kind])
        for g in range(KV_GROUPS):
            out_ref[0, kind, g] = res[g * N_CHUNKS:(g + 1) * N_CHUNKS, :].astype(out_ref.dtype)


def _cmp_prompt(kv, cw, *, n_seq):
    R = kv.shape[0]
    T = R // n_seq
    assert T == N_CHUNKS * CMP_STRIDE
    return pl.pallas_call(
        _cmp_prompt_body,
        grid=(n_seq,),
        in_specs=[pl.BlockSpec((T, LANES), lambda b: (b, 0)), pl.BlockSpec((T, LANES), lambda b: (b, 1)),
                  _const_spec(cw['wp'].shape), _const_spec(cw['pe'].shape), _const_spec(cw['w1'].shape),
                  _const_spec(cw['b1'].shape), _const_spec(cw['w2'].shape)],
        out_specs=pl.BlockSpec((1, 2, KV_GROUPS, N_CHUNKS, HEAD_DIM), lambda b: (b, 0, 0, 0, 0)),
        out_shape=jax.ShapeDtypeStruct((n_seq, 2, KV_GROUPS, N_CHUNKS, HEAD_DIM), jnp.bfloat16),
        compiler_params=pltpu.CompilerParams(
            dimension_semantics=("arbitrary",), vmem_limit_bytes=VMEM_LIMIT_BYTES),
        name="nsa_compress",
    )(kv, kv, cw['wp'], cw['pe'], cw['w1'], cw['b1'], cw['w2'])


ATTN_TQ = 128
SEL_KC = 512
K_SEL, V_SEL, K_WIN, V_WIN = 2, 3, 4, 5


def _split3(x):
    hi = x.astype(jnp.bfloat16)
    r = x - hi.astype(jnp.float32)
    mid = r.astype(jnp.bfloat16)
    lo = (r - mid.astype(jnp.float32)).astype(jnp.bfloat16)
    return hi, mid, lo


def _overlap_matrix():
    c = np.arange(LANES)[:, None]
    n = np.arange(LANES)[None, :]
    ov = ((c * CMP_STRIDE < n * SEL_BLOCK + SEL_BLOCK) & (c * CMP_STRIDE + CMP_BLOCK - 1 >= n * SEL_BLOCK)
          & (c < N_CHUNKS - 1))
    return jnp.asarray(ov, jnp.bfloat16)


def _expand_matrix(n_keys):
    return jnp.asarray(np.arange(LANES)[:, None] == (np.arange(n_keys) // SEL_BLOCK)[None, :], jnp.bfloat16)


def _cmp_branch(qg, kc, vc, cmask, ov, hpg, tq, token_major=False):
    rows = hpg * tq
    s = lax.dot_general(qg, kc, (((1,), (1,)), ((), ())), preferred_element_type=jnp.float32)
    if token_major:
        shape3, cm3, head_axis = (tq, hpg, LANES), cmask[:, None, :], 1
    else:
        shape3, cm3, head_axis = (hpg, tq, LANES), cmask[None], 0
    s3 = jnp.where(cm3, s.reshape(shape3), NEG_INF)
    m = jnp.max(s3, axis=-1, keepdims=True)
    e = jnp.where(cm3, jnp.exp(s3 - m), 0.0)
    p3 = e / jnp.maximum(jnp.sum(e, axis=-1, keepdims=True), 1e-30)
    o_c = jnp.dot(p3.reshape(rows, LANES).astype(jnp.bfloat16), vc, preferred_element_type=jnp.float32)
    psum = jnp.sum(p3, axis=head_axis)
    if tq < SUBLANES:
        psum = jnp.concatenate([psum, jnp.zeros((SUBLANES - tq, LANES), jnp.float32)], axis=0)
    hi, mid, lo = _split3(psum)
    imp = (jnp.dot(hi, ov, preferred_element_type=jnp.float32)
           + jnp.dot(mid, ov, preferred_element_type=jnp.float32)
           + jnp.dot(lo, ov, preferred_element_type=jnp.float32))
    return o_c, imp


def _block_select(imp, t_col, ns):
    blk = lax.broadcasted_iota(jnp.int32, imp.shape, 1)
    cur = t_col // SEL_BLOCK
    imp = jnp.where((blk == 0) | (blk == cur) | (blk == cur - 1), FORCE_SCORE, imp)
    imp = jnp.where(blk > cur, -1.0, imp)
    imp = jnp.where(blk >= ns, -2.0, imp)
    rank = jnp.zeros(imp.shape, jnp.int32)
    for mblk in range(ns):
        col = imp[:, mblk:mblk + 1]
        beats = (col > imp) | ((col == imp) & (blk > mblk))
        rank = rank + beats.astype(jnp.int32)
    return rank < min(SEL_TOPK, ns)


def _gate_col(g_ref, g, j, hpg):
    cols = [(g * hpg + h) * 3 + j for h in range(hpg)]
    return jnp.concatenate([g_ref[:, c:c + 1] for c in cols], axis=0)


def _attn_prompt_body(q_ref, kvh_ref, cmp_ref, g_ref, ov_ref, e_ref, o_ref, bias_ref):
    tq = q_ref.shape[2]
    T = kvh_ref.shape[2]
    hpg = HEADS_PER_GROUP
    rows = hpg * tq
    wk = WINDOW + tq
    t0 = pl.program_id(1) * tq
    t_col = t0 + lax.broadcasted_iota(jnp.int32, (tq, 1), 0)

    ws = pl.multiple_of(jnp.maximum(t0 - WINDOW, 0), tq)
    dist = t_col - (ws + lax.broadcasted_iota(jnp.int32, (1, wk), 1))
    bias_w = jnp.where((dist >= 0) & (dist <= WINDOW), 0.0, NEG_INF)
    cblk = lax.broadcasted_iota(jnp.int32, (1, LANES), 1)
    cmask = (cblk * CMP_STRIDE + (CMP_BLOCK - 1) <= t_col) & (cblk < N_CHUNKS - 1)
    kpos = lax.broadcasted_iota(jnp.int32, (1, T), 1)
    n_kc = (t0 + tq + SEL_KC - 1) // SEL_KC
    ov = ov_ref[...]

    for g in range(KV_GROUPS):
        qg = q_ref[0, g * hpg:(g + 1) * hpg].reshape(rows, HEAD_DIM)
        o_c, imp = _cmp_branch(qg, cmp_ref[0, 0, g], cmp_ref[0, 1, g], cmask, ov, hpg, tq)
        chosen = _block_select(imp, t_col, T // SEL_BLOCK)
        selk = jnp.dot(jnp.where(chosen, 1.0, 0.0).astype(jnp.bfloat16), e_ref[...],
                       preferred_element_type=jnp.float32)
        bias = jnp.where((selk > 0.5) & (kpos <= t_col), 0.0, NEG_INF)
        for c in range(T // SEL_KC):
            bias_ref[c] = bias[:, c * SEL_KC:(c + 1) * SEL_KC]

        def chunk(c, carry, g=g, qg=qg):
            m, l, acc = carry
            k0 = pl.multiple_of(c * SEL_KC, SEL_KC)
            k = kvh_ref[0, K_SEL * KV_GROUPS + g, pl.ds(k0, SEL_KC), :]
            v = kvh_ref[0, V_SEL * KV_GROUPS + g, pl.ds(k0, SEL_KC), :]
            s = lax.dot_general(qg, k, (((1,), (1,)), ((), ())), preferred_element_type=jnp.float32)
            s3 = s.reshape(hpg, tq, SEL_KC) + bias_ref[c][None]
            m_new = jnp.maximum(m, jnp.max(s3, axis=-1, keepdims=True))
            alpha = jnp.exp(m - m_new)
            p = jnp.exp(s3 - m_new)
            l = alpha * l + jnp.sum(p, axis=-1, keepdims=True)
            pv = jnp.dot(p.reshape(rows, SEL_KC).astype(jnp.bfloat16), v,
                         preferred_element_type=jnp.float32)
            return m_new, l, alpha.reshape(rows, 1) * acc + pv

        m, l, acc = lax.fori_loop(
            0, n_kc, chunk,
            (jnp.full((hpg, tq, 1), NEG_INF, jnp.float32), jnp.zeros((hpg, tq, 1), jnp.float32),
             jnp.zeros((rows, HEAD_DIM), jnp.float32)))
        o_s = acc / l.reshape(rows, 1)

        kw = kvh_ref[0, K_WIN * KV_GROUPS + g, pl.ds(ws, wk), :]
        vw = kvh_ref[0, V_WIN * KV_GROUPS + g, pl.ds(ws, wk), :]
        s = lax.dot_general(qg, kw, (((1,), (1,)), ((), ())), preferred_element_type=jnp.float32)
        s3 = s.reshape(hpg, tq, wk) + bias_w[None]
        p = jnp.exp(s3 - jnp.max(s3, axis=-1, keepdims=True))
        lw = jnp.sum(p, axis=-1, keepdims=True)
        o_w = jnp.dot(p.reshape(rows, wk).astype(jnp.bfloat16), vw,
                      preferred_element_type=jnp.float32) / lw.reshape(rows, 1)

        o = (_gate_col(g_ref, g, 0, hpg) * o_c + _gate_col(g_ref, g, 1, hpg) * o_s
             + _gate_col(g_ref, g, 2, hpg) * o_w)
        for h in range(hpg):
            c0 = (g * hpg + h) * HEAD_DIM
            o_ref[:, c0:c0 + HEAD_DIM] = o[h * tq:(h + 1) * tq, :].astype(o_ref.dtype)


def _attn_prompt(qh, kvh, cmp, gates):
    B, _, T, _ = qh.shape
    tq = min(ATTN_TQ, T)
    nq = T // tq
    ov = _overlap_matrix()
    ex = _expand_matrix(T)
    return pl.pallas_call(
        _attn_prompt_body,
        grid=(B, nq),
        in_specs=[pl.BlockSpec((1, N_HEADS, tq, HEAD_DIM), lambda b, i: (b, 0, i, 0)),
                  pl.BlockSpec((1, N_KV_KINDS * KV_GROUPS, T, HEAD_DIM), lambda b, i: (b, 0, 0, 0)),
                  pl.BlockSpec((1, 2, KV_GROUPS, N_CHUNKS, HEAD_DIM), lambda b, i: (b, 0, 0, 0, 0)),
                  pl.BlockSpec((tq, LANES), lambda b, i: (b * nq + i, 0)),
                  _const_spec(ov.shape), _const_spec(ex.shape)],
        out_specs=pl.BlockSpec((tq, NSA_Q_COLS), lambda b, i: (b * nq + i, 0)),
        out_shape=jax.ShapeDtypeStruct((B * T, NSA_Q_COLS), jnp.bfloat16),
        scratch_shapes=[pltpu.VMEM((T // SEL_KC, tq, SEL_KC), jnp.float32)],
        compiler_params=pltpu.CompilerParams(
            dimension_semantics=("arbitrary", "arbitrary"),
            vmem_limit_bytes=VMEM_LIMIT_BYTES),
        name="nsa_attn_prompt",
    )(qh, kvh, cmp, gates, ov, ex)


def _nsa_prompt_pallas(xp, w_in, cw):
    B, T, D = xp.shape
    _, kv, gates, qh, kvh = _nsa_proj(xp.reshape(B * T, D), w_in, n_seq=B, head_major=True)
    cmp = _cmp_prompt(kv, cw, n_seq=B)
    o = _attn_prompt(qh, kvh, cmp, gates)
    kv6 = kv.reshape(B, T, N_KV_KINDS, KV_GROUPS, HEAD_DIM)
    return o.reshape(B, T, NSA_Q_COLS), kv6[:, :, :4], kv6[:, T - min(WINDOW, T):, 4:6]


def _softmax_pv(qg, k, v, bias):
    s = lax.dot_general(qg, k, (((1,), (1,)), ((), ())), preferred_element_type=jnp.float32) + bias
    p = jnp.exp(s - jnp.max(s, axis=-1, keepdims=True))
    l = jnp.sum(p, axis=-1, keepdims=True)
    return jnp.dot(p.astype(jnp.bfloat16), v, preferred_element_type=jnp.float32) / l


def _rep_rows(x, n_tok, hpg):
    return jnp.concatenate([jnp.broadcast_to(x[s:s + 1], (hpg, x.shape[1])) for s in range(n_tok)], axis=0)


def _attn_sample_body(pt_ref, *refs, n_pages, n_tok):
    pages = refs[:n_pages]
    (win_ref, q_ref, kvn_ref, g_ref, wp_ref, pe_ref, w1_ref, b1_ref, w2_ref, ov_ref, e_ref,
     o_ref, kc_rows, vc_rows, selk, selv, wink, winv) = refs[n_pages:]
    del pt_ref
    hpg = HEADS_PER_GROUP
    page = pages[0].shape[1]
    past = n_pages * page
    wb = win_ref.shape[1]
    nk = selk.shape[1]
    nw = wink.shape[1]
    hd = HEAD_DIM

    @pl.when(pl.program_id(0) == 0)
    def _():
        selk[...] = jnp.zeros(selk.shape, selk.dtype)
        selv[...] = jnp.zeros(selv.shape, selv.dtype)
        wink[...] = jnp.zeros(wink.shape, wink.dtype)
        winv[...] = jnp.zeros(winv.shape, winv.dtype)

    bf = jnp.bfloat16
    for p in range(n_pages):
        r = slice(p * page, (p + 1) * page)
        kc_rows[r, :] = pages[p][0, :, 0:LANES]
        vc_rows[r, :] = pages[p][0, :, LANES:2 * LANES]
        for g in range(KV_GROUPS):
            selk[g, r, :] = pages[p][0, :, (K_SEL * KV_GROUPS + g) * hd:(K_SEL * KV_GROUPS + g + 1) * hd].astype(bf)
            selv[g, r, :] = pages[p][0, :, (V_SEL * KV_GROUPS + g) * hd:(V_SEL * KV_GROUPS + g + 1) * hd].astype(bf)
    kvn = kvn_ref[0]
    for g in range(KV_GROUPS):
        selk[g, past:past + n_tok, :] = kvn[:, (K_SEL * KV_GROUPS + g) * hd:(K_SEL * KV_GROUPS + g + 1) * hd].astype(bf)
        selv[g, past:past + n_tok, :] = kvn[:, (V_SEL * KV_GROUPS + g) * hd:(V_SEL * KV_GROUPS + g + 1) * hd].astype(bf)
        wink[g, 0:wb, :] = win_ref[0, :, g * hd:(g + 1) * hd].astype(bf)
        winv[g, 0:wb, :] = win_ref[0, :, (KV_GROUPS + g) * hd:(KV_GROUPS + g + 1) * hd].astype(bf)
        wink[g, wb:wb + n_tok, :] = kvn[:, (K_WIN * KV_GROUPS + g) * hd:(K_WIN * KV_GROUPS + g + 1) * hd].astype(bf)
        winv[g, wb:wb + n_tok, :] = kvn[:, (V_WIN * KV_GROUPS + g) * hd:(V_WIN * KV_GROUPS + g + 1) * hd].astype(bf)

    res = []
    for kind, rows_ref in enumerate((kc_rows, vc_rows)):
        cb = _cmp_bias(pe_ref.at[kind], w1_ref.at[kind], b1_ref.at[kind])
        res.append(_compress_rows(rows_ref, wp_ref.at[kind], cb, w2_ref.at[kind]).astype(bf))

    t_col = past + lax.broadcasted_iota(jnp.int32, (SUBLANES, 1), 0)
    cblk = lax.broadcasted_iota(jnp.int32, (1, LANES), 1)
    cmask = ((cblk * CMP_STRIDE + (CMP_BLOCK - 1) <= t_col) & (cblk < N_CHUNKS - 1))[:n_tok]
    kpos = lax.broadcasted_iota(jnp.int32, (1, nk), 1)
    wj = lax.broadcasted_iota(jnp.int32, (1, nw), 1)
    w_pos = past - wb + wj
    dist = t_col - w_pos
    bias_w = jnp.where((dist >= 0) & (dist <= WINDOW) & (w_pos >= 0) & (wj < wb + n_tok), 0.0, NEG_INF)
    bias_w = _rep_rows(bias_w, n_tok, hpg)
    ns = -(-(past + n_tok) // SEL_BLOCK)

    for g in range(KV_GROUPS):
        qg = q_ref[0, g]
        kc = res[0][g * N_CHUNKS:(g + 1) * N_CHUNKS, :]
        vc = res[1][g * N_CHUNKS:(g + 1) * N_CHUNKS, :]
        o_c, imp = _cmp_branch(qg, kc, vc, cmask, ov_ref[...], hpg, n_tok, token_major=True)
        chosen = _block_select(imp, t_col, ns)
        selm = jnp.dot(jnp.where(chosen, 1.0, 0.0).astype(bf), e_ref[...], preferred_element_type=jnp.float32)
        bias_s = _rep_rows(jnp.where((selm > 0.5) & (kpos <= t_col), 0.0, NEG_INF), n_tok, hpg)
        o_s = _softmax_pv(qg, selk[g], selv[g], bias_s)
        o_w = _softmax_pv(qg, wink[g], winv[g], bias_w)
        gt = g_ref[0, g]
        o_ref[0, g] = gt[:, 0:1] * o_c + gt[:, 1:2] * o_s + gt[:, 2:3] * o_w


def _attn_sample(pool, page_table, win, q, kvn, gates, cw):
    B, n_pages = page_table.shape
    page = pool.shape[1]
    n_tok = kvn.shape[1]
    rows = q.shape[2]
    past = n_pages * page
    assert past == N_CHUNKS * CMP_STRIDE
    wb = win.shape[1]
    nk = past + LANES
    nw = wb + LANES
    ov = _overlap_matrix()
    ex = _expand_matrix(nk)
    consts = (cw['wp'], cw['pe'], cw['w1'], cw['b1'], cw['w2'], ov, ex)
    cspec = lambda a: pl.BlockSpec(a.shape, lambda b, pt, nd=a.ndim: (0,) * nd, pipeline_mode=pl.Buffered(1))
    page_specs = [pl.BlockSpec((1, page, pool.shape[2]), lambda b, pt, p=p: (pt[b, p], 0, 0))
                  for p in range(n_pages)]
    grid_spec = pltpu.PrefetchScalarGridSpec(
        num_scalar_prefetch=1,
        grid=(B,),
        in_specs=page_specs + [
            pl.BlockSpec((1, wb, win.shape[2]), lambda b, pt: (b, 0, 0)),
            pl.BlockSpec((1, KV_GROUPS, rows, HEAD_DIM), lambda b, pt: (b, 0, 0, 0)),
            pl.BlockSpec((1, n_tok, kvn.shape[2]), lambda b, pt: (b, 0, 0)),
            pl.BlockSpec((1, KV_GROUPS, rows, gates.shape[3]), lambda b, pt: (b, 0, 0, 0)),
        ] + [cspec(a) for a in consts],
        out_specs=pl.BlockSpec((1, KV_GROUPS, rows, HEAD_DIM), lambda b, pt: (b, 0, 0, 0)),
        scratch_shapes=[pltpu.VMEM((past, LANES), jnp.float32), pltpu.VMEM((past, LANES), jnp.float32),
                        pltpu.VMEM((KV_GROUPS, nk, HEAD_DIM), jnp.bfloat16),
                        pltpu.VMEM((KV_GROUPS, nk, HEAD_DIM), jnp.bfloat16),
                        pltpu.VMEM((KV_GROUPS, nw, HEAD_DIM), jnp.bfloat16),
                        pltpu.VMEM((KV_GROUPS, nw, HEAD_DIM), jnp.bfloat16)])
    return pl.pallas_call(
        functools.partial(_attn_sample_body, n_pages=n_pages, n_tok=n_tok),
        grid_spec=grid_spec,
        out_shape=jax.ShapeDtypeStruct((B, KV_GROUPS, rows, HEAD_DIM), jnp.float32),
        compiler_params=pltpu.CompilerParams(
            dimension_semantics=("arbitrary",), vmem_limit_bytes=VMEM_LIMIT_BYTES),
        name="nsa_attn_sample",
    )(page_table, *([pool] * n_pages), win, q, kvn, gates, *consts)


def _nsa_sample_pallas(xs, kv_pool, page_table, win_buf, w_in, cw):
    S, B, D = xs.shape
    hpg = HEADS_PER_GROUP
    q, kv, gates = _nsa_proj(xs.reshape(S * B, D), w_in, n_seq=1, head_major=False)
    q_b = q.reshape(S, B, KV_GROUPS, hpg, HEAD_DIM).transpose(1, 2, 0, 3, 4).reshape(B, KV_GROUPS, S * hpg, HEAD_DIM)
    g_b = gates[:, :GATE_COLS].reshape(S, B, KV_GROUPS, hpg, 3).transpose(1, 2, 0, 3, 4)
    g_b = jnp.pad(g_b.reshape(B, KV_GROUPS, S * hpg, 3), ((0, 0), (0, 0), (0, 0), (0, SUBLANES - 3)))
    kvn = kv.reshape(S, B, NSA_KV_COLS).swapaxes(0, 1)
    n_pool, page = kv_pool.shape[:2]
    o = _attn_sample(kv_pool.reshape(n_pool, page, 4 * KV_GROUPS * HEAD_DIM), page_table,
                     win_buf.reshape(B, win_buf.shape[1], 2 * KV_GROUPS * HEAD_DIM), q_b, kvn, g_b, cw)
    o = o.reshape(B, KV_GROUPS, S, hpg, HEAD_DIM).transpose(2, 0, 1, 3, 4).reshape(S, B, NSA_Q_COLS)
    kv6 = kvn.reshape(B, S, N_KV_KINDS, KV_GROUPS, HEAD_DIM)
    win_new = jnp.concatenate([win_buf[:, S:], kv6[:, :, 4:6]], axis=1)
    return o.astype(jnp.bfloat16), kv6[:, :, :4], win_new


def kernel(x_prompt, x_sample, cache_nsa_kv, cache_nsa_win, state_rglru_h, state_rglru_conv,
           state_ffn_conv, page_table, p_prompt, p_sample,
           nsa_w_in, nsa_w_out, nsa_cmp_pe, nsa_cmp_w1, nsa_cmp_b1, nsa_cmp_w2,
           rg_w_in, rg_conv_w, rg_conv_b, rg_gate_a_w, rg_gate_a_b, rg_gate_x_w, rg_gate_x_b,
           rg_lambda, rg_w_out,
           ffn_w_up, ffn_conv_w, ffn_conv_b, ffn_w_down,
           ln_mix_g, ln_mix_b, ln_ffn_g, ln_ffn_b, ple_w_proj, ple_w_gate):
    xp = x_prompt
    xs = x_sample.swapaxes(0, 1)
    kv_p, kv_s, win_p, win_s = [], [], [], []
    h_p, h_s, rc_p, rc_s = [], [], [], []
    fc_p, fc_s = [], []
    for i in range(DEPTH):
        j = i // N_MIXERS
        if i % N_MIXERS == 0:
            cw = _cmp_weights(nsa_cmp_pe[j], nsa_cmp_w1[j], nsa_cmp_b1[j], nsa_cmp_w2[j])
            o_p, rows_p, wnd_p = _nsa_prompt_pallas(xp, nsa_w_in[j], cw)
            o_s, rows_s, wnd_s = _nsa_sample_pallas(xs, cache_nsa_kv[j], page_table, cache_nsa_win[j],
                                                    nsa_w_in[j], cw)
            w_out = nsa_w_out[j]
            kv_p.append(rows_p)
            kv_s.append(rows_s)
            win_p.append(wnd_p)
            win_s.append(wnd_s)
        else:
            rgw = _rg_weights(rg_w_in[j], rg_conv_w[j], rg_conv_b[j], rg_gate_a_w[j], rg_gate_a_b[j],
                              rg_gate_x_w[j], rg_gate_x_b[j], rg_lambda[j])
            o_p, hl_p, cs_p = _rglru_prompt(xp, rgw)
            o_s, hl_s, cs_s = _rglru_sample(xs, state_rglru_h[j], state_rglru_conv[j], rgw)
            w_out = rg_w_out[j]
            h_p.append(hl_p)
            h_s.append(hl_s)
            rc_p.append(cs_p)
            rc_s.append(cs_s)
        w = _tail_weights(w_out, ln_mix_g[i], ln_mix_b[i], ln_ffn_g[i], ln_ffn_b[i], ffn_w_up[i],
                          ffn_conv_w[i], ffn_conv_b[i], ffn_w_down[i], ple_w_proj[i], ple_w_gate[i])
        xp, fs_p = _layer_tail_prompt(xp, o_p, p_prompt[i], w)
        xs, fs_s = _layer_tail_sample(xs, o_s, p_sample[i].swapaxes(0, 1), state_ffn_conv[i], w)
        fc_p.append(fs_p)
        fc_s.append(fs_s)
    return (xp, xs.swapaxes(0, 1),
            jnp.stack(kv_p), jnp.stack(kv_s),
            jnp.stack(win_p), jnp.stack(win_s),
            jnp.stack(h_p), jnp.stack(h_s),
            jnp.stack(rc_p), jnp.stack(rc_s),
            jnp.stack(fc_p), jnp.stack(fc_s))
```

```python
import functools

import jax
import jax.numpy as jnp
import numpy as np
from jax import lax
from jax.experimental import pallas as pl
from jax.experimental.pallas import tpu as pltpu

D_MODEL = 1024
DEPTH = 4
N_MIXERS = 2
N_HEADS = 16
HEAD_DIM = D_MODEL // N_HEADS
KV_GROUPS = 2
HEADS_PER_GROUP = N_HEADS // KV_GROUPS
CMP_BLOCK = 32
CMP_STRIDE = 16
SEL_BLOCK = 64
SEL_TOPK = 16
WINDOW = 512
Q_BLOCK = 64
NSA_Q_COLS = N_HEADS * HEAD_DIM
NSA_KV_COLS = 6 * KV_GROUPS * HEAD_DIM
ATTN_SCALE = HEAD_DIM ** -0.5
FORCE_SCORE = 1e6
NEG_INF = -1e30
D_RNN = D_MODEL
RG_BLOCKS = 4
RG_BLOCK_DIM = D_RNN // RG_BLOCKS
RG_CONV = 4
RG_C = 8.0
D_FF = 3 * D_MODEL
FFN_CONV = 3
ALPHA = (2 * DEPTH) ** 0.25
LN_EPS = 1e-5

VMEM_LIMIT_BYTES = 56 * 1024 * 1024
SUBLANES = 8
FF_CHUNK = 512
TAIL_TM = 512


def _const_spec(shape):
    nd = len(shape)
    return pl.BlockSpec(shape, lambda *_: (0,) * nd, pipeline_mode=pl.Buffered(1))


def _layer_norm(x, g, b):
    mu = jnp.mean(x, axis=-1, keepdims=True)
    xc = x - mu
    var = jnp.mean(xc * xc, axis=-1, keepdims=True)
    return xc * lax.rsqrt(var + LN_EPS) * g + b


def _tail_body(x_ref, o_ref, p_ref, halo_ref, wout_ref, wup_ref, wdown_ref, wgate_ref, wproj_ref,
               ln_ref, cw_ref, cb_ref, out_ref, st_ref, tmp_ref, acc_ref, *, halo, shift):
    tm = x_ref.shape[0]
    n_chunks = wdown_ref.shape[0] // FF_CHUNK
    cw = 2 * FF_CHUNK

    @pl.when(pl.program_id(1) == 0)
    def _():
        st_ref[...] = halo_ref[...]

    x = x_ref[...]
    mix = jnp.dot(o_ref[...], wout_ref[...], preferred_element_type=jnp.float32)
    h = _layer_norm(ALPHA * x + mix, ln_ref[0:1, :], ln_ref[1:2, :])
    h_bf = h.astype(jnp.bfloat16)

    for c in range(n_chunks):
        cols = slice(c * cw, (c + 1) * cw)
        u = jnp.dot(h_bf, wup_ref[:, cols], preferred_element_type=jnp.float32)
        tmp_ref[0:halo, :] = st_ref[:, cols]
        tmp_ref[halo:halo + tm, :] = u
        st_ref[:, cols] = u[tm - halo:tm, :]
        p1 = tmp_ref[halo - shift:halo - shift + tm, :]
        p2 = tmp_ref[halo - 2 * shift:halo - 2 * shift + tm, :]
        cv = (cb_ref[:, cols] + p2 * cw_ref[0:1, cols] + p1 * cw_ref[1:2, cols]
              + u * cw_ref[2:3, cols])
        act = jax.nn.gelu(cv[:, :FF_CHUNK]) * cv[:, FF_CHUNK:]
        part = jnp.dot(act.astype(jnp.bfloat16), wdown_ref[c * FF_CHUNK:(c + 1) * FF_CHUNK, :],
                       preferred_element_type=jnp.float32)
        if c == 0:
            acc_ref[...] = part
        else:
            acc_ref[...] += part

    y = _layer_norm(ALPHA * h + acc_ref[...], ln_ref[2:3, :], ln_ref[3:4, :])
    gate = jax.nn.sigmoid(jnp.dot(y.astype(jnp.bfloat16), wgate_ref[...],
                                  preferred_element_type=jnp.float32))
    pp = jnp.dot(p_ref[...].astype(jnp.bfloat16), wproj_ref[...],
                 preferred_element_type=jnp.float32)
    out_ref[...] = y + gate * pp


def _ff_permute(a):
    lead = a.shape[:-1]
    return a.reshape(*lead, 2, -1, FF_CHUNK).swapaxes(-3, -2).reshape(*lead, -1)


def _ff_unpermute(a):
    lead = a.shape[:-1]
    return a.reshape(*lead, -1, 2, FF_CHUNK).swapaxes(-3, -2).reshape(*lead, -1)


def _tail_call(x, o, p, halo0, w, *, n_seq, tm, halo, shift):
    R, D = x.shape
    T = R // n_seq
    nt = T // tm
    F2 = w['up'].shape[1]
    row = lambda b, t: (b * nt + t, 0)
    seq = lambda b, t: (b, 0)
    return pl.pallas_call(
        functools.partial(_tail_body, halo=halo, shift=shift),
        grid=(n_seq, nt),
        in_specs=[pl.BlockSpec((tm, D), row),
                  pl.BlockSpec((tm, o.shape[1]), row),
                  pl.BlockSpec((tm, p.shape[1]), row),
                  pl.BlockSpec((halo, F2), seq),
                  _const_spec(w['out'].shape), _const_spec(w['up'].shape),
                  _const_spec(w['down'].shape), _const_spec(w['gate'].shape),
                  _const_spec(w['proj'].shape), _const_spec(w['ln'].shape),
                  _const_spec(w['cw'].shape), _const_spec(w['cb'].shape)],
        out_specs=[pl.BlockSpec((tm, D), row),
                   pl.BlockSpec((halo, F2), seq)],
        out_shape=[jax.ShapeDtypeStruct((R, D), jnp.float32),
                   jax.ShapeDtypeStruct((n_seq * halo, F2), jnp.float32)],
        scratch_shapes=[pltpu.VMEM((halo + tm, 2 * FF_CHUNK), jnp.float32),
                        pltpu.VMEM((tm, D), jnp.float32)],
        compiler_params=pltpu.CompilerParams(
            dimension_semantics=("arbitrary", "arbitrary"),
            vmem_limit_bytes=VMEM_LIMIT_BYTES),
        name="layer_tail",
    )(x, o, p, halo0, w['out'], w['up'], w['down'], w['gate'], w['proj'], w['ln'], w['cw'], w['cb'])


def _tail_weights(w_out, ln_mg, ln_mb, ln_fg, ln_fb, w_up, conv_w, conv_b, w_down, w_proj, w_gate):
    bf = jnp.bfloat16
    return dict(out=w_out.astype(bf), up=_ff_permute(w_up).astype(bf), down=w_down.astype(bf),
                gate=w_gate.astype(bf), proj=w_proj.astype(bf),
                ln=jnp.stack([ln_mg, ln_mb, ln_fg, ln_fb]),
                cw=_ff_permute(conv_w), cb=_ff_permute(conv_b)[None, :])


def _layer_tail_prompt(xp, op, pp, w):
    B, T, D = xp.shape
    F2 = w['up'].shape[1]
    halo0 = jnp.zeros((B * SUBLANES, F2), jnp.float32)
    y, st = _tail_call(xp.reshape(B * T, D), op.reshape(B * T, -1), pp.reshape(B * T, -1), halo0, w,
                       n_seq=B, tm=min(TAIL_TM, T), halo=SUBLANES, shift=1)
    st = st.reshape(B, SUBLANES, F2)[:, SUBLANES - (FFN_CONV - 1):, :]
    return y.reshape(B, T, D), _ff_unpermute(st)


def _layer_tail_sample(xs, os_, ps, state, w):
    S, B, D = xs.shape
    F2 = w['up'].shape[1]
    halo = (FFN_CONV - 1) * B
    halo0 = _ff_permute(state).transpose(1, 0, 2).reshape(halo, F2)
    y, st = _tail_call(xs.reshape(S * B, D), os_.reshape(S * B, -1), ps.reshape(S * B, -1), halo0, w,
                       n_seq=1, tm=S * B, halo=halo, shift=B)
    st = st.reshape(FFN_CONV - 1, B, F2).transpose(1, 0, 2)
    return y.reshape(S, B, D), _ff_unpermute(st)


RG_TM = 256


def _rg_body(x_ref, halo_ref, h0_ref, win_ref, cw_ref, cb_ref, gaw_ref, gxw_ref, vec_ref,
             o_ref, st_ref, hl_ref, tmp_ref, *, halo, shift):
    tm = x_ref.shape[0]
    dr = o_ref.shape[1]
    nblk = gaw_ref.shape[0]
    bd = dr // nblk

    @pl.when(pl.program_id(1) == 0)
    def _():
        st_ref[...] = halo_ref[...]
        hl_ref[...] = h0_ref[...]

    u = jnp.dot(x_ref[...].astype(jnp.bfloat16), win_ref[...], preferred_element_type=jnp.float32)
    gate_br = u[:, :dr]
    rec_br = u[:, dr:]
    tmp_ref[0:halo, :] = st_ref[...]
    tmp_ref[halo:halo + tm, :] = rec_br
    st_ref[...] = rec_br[tm - halo:tm, :]
    rec = cb_ref[...] + rec_br * cw_ref[RG_CONV - 1:RG_CONV, :]
    for k in range(1, RG_CONV):
        lo = halo - k * shift
        rec = rec + tmp_ref[lo:lo + tm, :] * cw_ref[RG_CONV - 1 - k:RG_CONV - k, :]

    rec_bf = rec.astype(jnp.bfloat16)
    ga = jnp.concatenate([jnp.dot(rec_bf[:, n * bd:(n + 1) * bd], gaw_ref[n],
                                  preferred_element_type=jnp.float32) for n in range(nblk)], axis=1)
    gx = jnp.concatenate([jnp.dot(rec_bf[:, n * bd:(n + 1) * bd], gxw_ref[n],
                                  preferred_element_type=jnp.float32) for n in range(nblk)], axis=1)
    r = jax.nn.sigmoid(ga + vec_ref[0:1, :])
    i = jax.nn.sigmoid(gx + vec_ref[1:2, :])
    log_a = (-RG_C) * r * jax.nn.softplus(-vec_ref[2:3, :])
    a = jnp.exp(log_a)
    b = jnp.sqrt(-jnp.tanh(log_a) * (a * a + 1.0)) * i * rec

    row = lax.broadcasted_iota(jnp.int32, (tm, 1), 0)
    span = SUBLANES if shift == 1 else tm
    d = shift
    while d < span:
        keep = (row % span) >= d
        a_sh = jnp.where(keep, pltpu.roll(a, d, 0), 1.0)
        b_sh = jnp.where(keep, pltpu.roll(b, d, 0), 0.0)
        b = b + a * b_sh
        a = a * a_sh
        d *= 2
    if shift == 1:
        carry = hl_ref[0:1, :]
        groups = []
        for j in range(tm // span):
            hj = b[j * span:(j + 1) * span, :] + a[j * span:(j + 1) * span, :] * carry
            carry = hj[span - 1:span, :]
            groups.append(hj)
        hs = jnp.concatenate(groups, axis=0)
        hl_ref[...] = jnp.broadcast_to(carry, hl_ref.shape)
    else:
        carry = hl_ref[0:shift, :]
        hs = b + a * jnp.tile(carry, (tm // shift, 1))
        hl_ref[...] = hs[tm - shift:tm, :]
    o_ref[...] = (hs * jax.nn.gelu(gate_br)).astype(o_ref.dtype)


def _rg_call(x, halo0, h0, w, *, n_seq, tm, halo, shift):
    R, D = x.shape
    T = R // n_seq
    nt = T // tm
    dr = w['cw'].shape[1]
    hrows = max(shift, SUBLANES)
    row = lambda b, t: (b * nt + t, 0)
    seq = lambda b, t: (b, 0)
    return pl.pallas_call(
        functools.partial(_rg_body, halo=halo, shift=shift),
        grid=(n_seq, nt),
        in_specs=[pl.BlockSpec((tm, D), row),
                  pl.BlockSpec((halo, dr), seq),
                  pl.BlockSpec((hrows, dr), seq),
                  _const_spec(w['in'].shape), _const_spec(w['cw'].shape), _const_spec(w['cb'].shape),
                  _const_spec(w['ga'].shape), _const_spec(w['gx'].shape), _const_spec(w['vec'].shape)],
        out_specs=[pl.BlockSpec((tm, dr), row),
                   pl.BlockSpec((halo, dr), seq),
                   pl.BlockSpec((hrows, dr), seq)],
        out_shape=[jax.ShapeDtypeStruct((R, dr), jnp.bfloat16),
                   jax.ShapeDtypeStruct((n_seq * halo, dr), jnp.float32),
                   jax.ShapeDtypeStruct((n_seq * hrows, dr), jnp.float32)],
        scratch_shapes=[pltpu.VMEM((halo + tm, dr), jnp.float32)],
        compiler_params=pltpu.CompilerParams(
            dimension_semantics=("arbitrary", "arbitrary"),
            vmem_limit_bytes=VMEM_LIMIT_BYTES),
        name="rglru",
    )(x, halo0, h0, w['in'], w['cw'], w['cb'], w['ga'], w['gx'], w['vec'])


def _rg_weights(w_in, conv_w, conv_b, ga_w, ga_b, gx_w, gx_b, lam):
    bf = jnp.bfloat16
    return dict(**{'in': w_in.astype(bf)}, cw=conv_w, cb=conv_b[None, :], ga=ga_w.astype(bf),
                gx=gx_w.astype(bf), vec=jnp.stack([ga_b, gx_b, lam]))


def _rglru_prompt(xp, w):
    B, T, D = xp.shape
    dr = w['cw'].shape[1]
    o, st, hl = _rg_call(xp.reshape(B * T, D), jnp.zeros((B * SUBLANES, dr), jnp.float32),
                         jnp.zeros((B * SUBLANES, dr), jnp.float32), w,
                         n_seq=B, tm=min(RG_TM, T), halo=SUBLANES, shift=1)
    st = st.reshape(B, SUBLANES, dr)[:, SUBLANES - (RG_CONV - 1):, :]
    return o.reshape(B, T, dr), hl.reshape(B, SUBLANES, dr)[:, 0, :], st


def _rglru_sample(xs, h0, conv_prev, w):
    S, B, D = xs.shape
    dr = w['cw'].shape[1]
    halo = (RG_CONV - 1) * B
    o, st, hl = _rg_call(xs.reshape(S * B, D), conv_prev.swapaxes(0, 1).reshape(halo, dr), h0, w,
                         n_seq=1, tm=S * B, halo=halo, shift=B)
    return o.reshape(S, B, dr), hl, st.reshape(RG_CONV - 1, B, dr).swapaxes(0, 1)


LANES = 128
NSA_PROJ_TM = 512
N_KV_KINDS = 6
GATE_COLS = 3 * N_HEADS


def _nsa_proj_body(x_ref, w_ref, q_ref, kv_ref, g_ref, *rest, head_major):
    proj = jnp.dot(x_ref[...].astype(jnp.bfloat16), w_ref[...], preferred_element_type=jnp.float32)
    q = proj[:, :NSA_Q_COLS] * ATTN_SCALE
    kv = proj[:, NSA_Q_COLS:NSA_Q_COLS + NSA_KV_COLS]
    q_ref[...] = q.astype(q_ref.dtype)
    kv_ref[...] = kv
    g_ref[...] = jax.nn.sigmoid(proj[:, NSA_Q_COLS + NSA_KV_COLS:])
    if head_major:
        qh_ref, kvh_ref = rest
        for h in range(N_HEADS):
            qh_ref[0, h] = q[:, h * HEAD_DIM:(h + 1) * HEAD_DIM].astype(qh_ref.dtype)
        for k in range(N_KV_KINDS * KV_GROUPS):
            kvh_ref[0, k] = kv[:, k * HEAD_DIM:(k + 1) * HEAD_DIM].astype(kvh_ref.dtype)


def _nsa_proj(x, w_in, *, n_seq, head_major):
    R, D = x.shape
    T = R // n_seq
    tm = min(NSA_PROJ_TM, T)
    nt = T // tm
    n_in = w_in.shape[1]
    n_pad = NSA_Q_COLS + NSA_KV_COLS + LANES
    w = jnp.pad(w_in, ((0, 0), (0, n_pad - n_in))).astype(jnp.bfloat16)
    row = lambda b, t: (b * nt + t, 0)
    out_specs = [pl.BlockSpec((tm, NSA_Q_COLS), row), pl.BlockSpec((tm, NSA_KV_COLS), row),
                 pl.BlockSpec((tm, LANES), row)]
    out_shape = [jax.ShapeDtypeStruct((R, NSA_Q_COLS), jnp.bfloat16),
                 jax.ShapeDtypeStruct((R, NSA_KV_COLS), jnp.float32),
                 jax.ShapeDtypeStruct((R, LANES), jnp.float32)]
    if head_major:
        hm = lambda b, t: (b, 0, t, 0)
        out_specs += [pl.BlockSpec((1, N_HEADS, tm, HEAD_DIM), hm),
                      pl.BlockSpec((1, N_KV_KINDS * KV_GROUPS, tm, HEAD_DIM), hm)]
        out_shape += [jax.ShapeDtypeStruct((n_seq, N_HEADS, T, HEAD_DIM), jnp.bfloat16),
                      jax.ShapeDtypeStruct((n_seq, N_KV_KINDS * KV_GROUPS, T, HEAD_DIM), jnp.bfloat16)]
    return pl.pallas_call(
        functools.partial(_nsa_proj_body, head_major=head_major),
        grid=(n_seq, nt),
        in_specs=[pl.BlockSpec((tm, D), row), _const_spec(w.shape)],
        out_specs=out_specs,
        out_shape=out_shape,
        compiler_params=pltpu.CompilerParams(
            dimension_semantics=("arbitrary", "arbitrary"),
            vmem_limit_bytes=VMEM_LIMIT_BYTES),
        name="nsa_proj",
    )(x, w)


N_CHUNKS = 128
CMP_HALF = CMP_BLOCK // CMP_STRIDE


def _compress_rows(rows_ref, wp_ref, cb, w2_ref):
    n = N_CHUNKS
    hid = w2_ref.shape[0]
    lane = lax.broadcasted_iota(jnp.int32, (n, LANES), 1)
    low = lane < HEAD_DIM
    z = jnp.zeros((KV_GROUPS * n, 2 * hid), jnp.float32)
    for j in range(CMP_STRIDE // 2):
        va = rows_ref[pl.ds(2 * j, n, stride=CMP_STRIDE), :]
        vb = rows_ref[pl.ds(2 * j + 1, n, stride=CMP_STRIDE), :]
        g0 = jnp.where(low, va, pltpu.roll(vb, HEAD_DIM, 1))
        g1 = jnp.where(low, pltpu.roll(va, HEAD_DIM, 1), vb)
        lhs = jnp.concatenate([g0, g1], axis=0).astype(jnp.bfloat16)
        z = z + jnp.dot(lhs, wp_ref[j], preferred_element_type=jnp.float32)
    za = z[:, :hid]
    zb = pltpu.roll(z[:, hid:], KV_GROUPS * n - 1, 0)
    h = jax.nn.gelu(za + zb + cb)
    return jnp.dot(h.astype(jnp.bfloat16), w2_ref[...], preferred_element_type=jnp.float32)


def _cmp_bias_init(cb_ref, pe_ref, w1_ref, b1_ref):
    @pl.when(pl.program_id(0) == 0)
    def _():
        for kind in range(2):
            cb = b1_ref[kind] + jnp.sum(pe_ref[kind] * w1_ref[kind], axis=0, keepdims=True)
            cb_ref[kind] = jnp.broadcast_to(cb, cb_ref.shape[1:])


def _cmp_weights(pe, w1, b1, w2):
    kinds, L, hd, hid = w1.shape
    half = L // 2
    wa = w1[:, :half].reshape(kinds, half // 2, 2 * hd, hid)
    wb = w1[:, half:].reshape(kinds, half // 2, 2 * hd, hid)
    return dict(wp=jnp.concatenate([wa, wb], axis=-1).astype(jnp.bfloat16),
                pe=pe.reshape(kinds, L * hd, 1), w1=w1.reshape(kinds, L * hd, hid),
                b1=b1.reshape(kinds, 1, hid), w2=w2.astype(jnp.bfloat16))


def _cmp_prompt_body(krows_ref, vrows_ref, wp_ref, pe_ref, w1_ref, b1_ref, w2_ref, out_ref, cb_ref):
    _cmp_bias_init(cb_ref, pe_ref, w1_ref, b1_ref)
    for kind, rows_ref in enumerate((krows_ref, vrows_ref)):
        res = _compress_rows(rows_ref, wp_ref.at[kind], cb_ref[kind, 0:1, :], w2_ref.at[kind])
        for g in range(KV_GROUPS):
            out_ref[0, kind, g] = res[g * N_CHUNKS:(g + 1) * N_CHUNKS, :].astype(out_ref.dtype)


def _cmp_prompt(kv, cw, *, n_seq):
    R = kv.shape[0]
    T = R // n_seq
    assert T == N_CHUNKS * CMP_STRIDE
    return pl.pallas_call(
        _cmp_prompt_body,
        grid=(n_seq,),
        in_specs=[pl.BlockSpec((T, LANES), lambda b: (b, 0)), pl.BlockSpec((T, LANES), lambda b: (b, 1)),
                  _const_spec(cw['wp'].shape), _const_spec(cw['pe'].shape), _const_spec(cw['w1'].shape),
                  _const_spec(cw['b1'].shape), _const_spec(cw['w2'].shape)],
        out_specs=pl.BlockSpec((1, 2, KV_GROUPS, N_CHUNKS, HEAD_DIM), lambda b: (b, 0, 0, 0, 0)),
        out_shape=jax.ShapeDtypeStruct((n_seq, 2, KV_GROUPS, N_CHUNKS, HEAD_DIM), jnp.bfloat16),
        scratch_shapes=[pltpu.VMEM((2, SUBLANES, cw['w2'].shape[1]), jnp.float32)],
        compiler_params=pltpu.CompilerParams(
            dimension_semantics=("arbitrary",), vmem_limit_bytes=VMEM_LIMIT_BYTES),
        name="nsa_compress",
    )(kv, kv, cw['wp'], cw['pe'], cw['w1'], cw['b1'], cw['w2'])


ATTN_TQ = 128
SEL_KC = 512
K_SEL, V_SEL, K_WIN, V_WIN = 2, 3, 4, 5


def _split3(x):
    hi = x.astype(jnp.bfloat16)
    r = x - hi.astype(jnp.float32)
    mid = r.astype(jnp.bfloat16)
    lo = (r - mid.astype(jnp.float32)).astype(jnp.bfloat16)
    return hi, mid, lo


def _overlap_matrix():
    c = np.arange(LANES)[:, None]
    n = np.arange(LANES)[None, :]
    ov = ((c * CMP_STRIDE < n * SEL_BLOCK + SEL_BLOCK) & (c * CMP_STRIDE + CMP_BLOCK - 1 >= n * SEL_BLOCK)
          & (c < N_CHUNKS - 1))
    return jnp.asarray(ov, jnp.bfloat16)


def _expand_matrix(n_keys):
    return jnp.asarray(np.arange(LANES)[:, None] == (np.arange(n_keys) // SEL_BLOCK)[None, :], jnp.bfloat16)


def _cmp_branch(qg, kc, vc, cmask, ov, hpg, tq, token_major=False):
    rows = hpg * tq
    s = lax.dot_general(qg, kc, (((1,), (1,)), ((), ())), preferred_element_type=jnp.float32)
    if token_major:
        shape3, cm3, head_axis = (tq, hpg, LANES), cmask[:, None, :], 1
    else:
        shape3, cm3, head_axis = (hpg, tq, LANES), cmask[None], 0
    s3 = jnp.where(cm3, s.reshape(shape3), NEG_INF)
    m = jnp.max(s3, axis=-1, keepdims=True)
    e = jnp.where(cm3, jnp.exp(s3 - m), 0.0)
    p3 = e / jnp.maximum(jnp.sum(e, axis=-1, keepdims=True), 1e-30)
    o_c = jnp.dot(p3.reshape(rows, LANES).astype(jnp.bfloat16), vc, preferred_element_type=jnp.float32)
    psum = jnp.sum(p3, axis=head_axis)
    if tq < SUBLANES:
        psum = jnp.concatenate([psum, jnp.zeros((SUBLANES - tq, LANES), jnp.float32)], axis=0)
    hi, mid, lo = _split3(psum)
    imp = (jnp.dot(hi, ov, preferred_element_type=jnp.float32)
           + jnp.dot(mid, ov, preferred_element_type=jnp.float32)
           + jnp.dot(lo, ov, preferred_element_type=jnp.float32))
    return o_c, imp


def _block_select(imp, t_col, ns):
    blk = lax.broadcasted_iota(jnp.int32, imp.shape, 1)
    cur = t_col // SEL_BLOCK
    imp = jnp.where((blk == 0) | (blk == cur) | (blk == cur - 1), FORCE_SCORE, imp)
    imp = jnp.where(blk > cur, -1.0, imp)
    imp = jnp.where(blk >= ns, -2.0, imp)
    rank = jnp.zeros(imp.shape, jnp.int32)
    for mblk in range(ns):
        col = imp[:, mblk:mblk + 1]
        beats = (col > imp) | ((col == imp) & (blk > mblk))
        rank = rank + beats.astype(jnp.int32)
    return rank < min(SEL_TOPK, ns)


def _block_select_t(imp, t_row, ns):
    ns_pad = -(-ns // SUBLANES) * SUBLANES
    it = imp.T[:ns_pad]
    blk = lax.broadcasted_iota(jnp.int32, it.shape, 0)
    cur = t_row // SEL_BLOCK
    it = jnp.where((blk == 0) | (blk == cur) | (blk == cur - 1), FORCE_SCORE, it)
    it = jnp.where(blk > cur, -1.0, it)
    it = jnp.where(blk >= ns, -2.0, it)
    rank = jnp.zeros(it.shape, jnp.int32)
    for mblk in range(ns):
        row = it[mblk:mblk + 1, :]
        beats = (row > it) | ((row == it) & (blk > mblk))
        rank = rank + beats.astype(jnp.int32)
    chosen_t = jnp.where(rank < min(SEL_TOPK, ns), 1.0, 0.0)
    chosen_t = jnp.concatenate([chosen_t, jnp.zeros((LANES - ns_pad, LANES), jnp.float32)], axis=0)
    return chosen_t.T


def _gate_col(g_ref, g, j, hpg):
    cols = [(g * hpg + h) * 3 + j for h in range(hpg)]
    return jnp.concatenate([g_ref[:, c:c + 1] for c in cols], axis=0)


def _attn_prompt_body(q_ref, kvh_ref, cmp_ref, g_ref, ov_ref, e_ref, o_ref, bias_ref):
    tq = q_ref.shape[2]
    T = kvh_ref.shape[2]
    hpg = HEADS_PER_GROUP
    rows = hpg * tq
    wk = WINDOW + tq
    t0 = pl.program_id(1) * tq
    t_col = t0 + lax.broadcasted_iota(jnp.int32, (tq, 1), 0)

    ws = pl.multiple_of(jnp.maximum(t0 - WINDOW, 0), tq)
    dist = t_col - (ws + lax.broadcasted_iota(jnp.int32, (1, wk), 1))
    bias_w = jnp.where((dist >= 0) & (dist <= WINDOW), 0.0, NEG_INF)
    cblk = lax.broadcasted_iota(jnp.int32, (1, LANES), 1)
    cmask = (cblk * CMP_STRIDE + (CMP_BLOCK - 1) <= t_col) & (cblk < N_CHUNKS - 1)
    kpos = lax.broadcasted_iota(jnp.int32, (1, T), 1)
    n_kc = (t0 + tq + SEL_KC - 1) // SEL_KC
    ov = ov_ref[...]
    ns = T // SEL_BLOCK
    t_row = t0 + lax.broadcasted_iota(jnp.int32, (1, tq), 1)
    causal_blocks = jnp.where(lax.broadcasted_iota(jnp.int32, (tq, LANES), 1) <= t_col // SEL_BLOCK, 1.0, 0.0)

    for g in range(KV_GROUPS):
        qg = q_ref[0, g * hpg:(g + 1) * hpg].reshape(rows, HEAD_DIM)
        o_c, imp = _cmp_branch(qg, cmp_ref[0, 0, g], cmp_ref[0, 1, g], cmask, ov, hpg, tq)
        if tq == LANES:
            chosen = lax.cond((t0 + tq - 1) // SEL_BLOCK + 1 > min(SEL_TOPK, ns),
                              lambda imp=imp: _block_select_t(imp, t_row, ns),
                              lambda: causal_blocks)
        else:
            chosen = jnp.where(_block_select(imp, t_col, ns), 1.0, 0.0)
        selk = jnp.dot(chosen.astype(jnp.bfloat16), e_ref[...], preferred_element_type=jnp.float32)
        bias = jnp.where((selk > 0.5) & (kpos <= t_col), 0.0, NEG_INF)
        for c in range(T // SEL_KC):
            bias_ref[c] = bias[:, c * SEL_KC:(c + 1) * SEL_KC]

        def chunk(c, carry, g=g, qg=qg):
            m, l, acc = carry
            k0 = pl.multiple_of(c * SEL_KC, SEL_KC)
            k = kvh_ref[0, K_SEL * KV_GROUPS + g, pl.ds(k0, SEL_KC), :]
            v = kvh_ref[0, V_SEL * KV_GROUPS + g, pl.ds(k0, SEL_KC), :]
            s = lax.dot_general(qg, k, (((1,), (1,)), ((), ())), preferred_element_type=jnp.float32)
            s3 = s.reshape(hpg, tq, SEL_KC) + bias_ref[c][None]
            m_new = jnp.maximum(m, jnp.max(s3, axis=-1, keepdims=True))
            alpha = jnp.exp(m - m_new)
            p = jnp.exp(s3 - m_new)
            l = alpha * l + jnp.sum(p, axis=-1, keepdims=True)
            pv = jnp.dot(p.reshape(rows, SEL_KC).astype(jnp.bfloat16), v,
                         preferred_element_type=jnp.float32)
            return m_new, l, alpha.reshape(rows, 1) * acc + pv

        m, l, acc = lax.fori_loop(
            0, n_kc, chunk,
            (jnp.full((hpg, tq, 1), NEG_INF, jnp.float32), jnp.zeros((hpg, tq, 1), jnp.float32),
             jnp.zeros((rows, HEAD_DIM), jnp.float32)))
        o_s = acc / l.reshape(rows, 1)

        kw = kvh_ref[0, K_WIN * KV_GROUPS + g, pl.ds(ws, wk), :]
        vw = kvh_ref[0, V_WIN * KV_GROUPS + g, pl.ds(ws, wk), :]
        s = lax.dot_general(qg, kw, (((1,), (1,)), ((), ())), preferred_element_type=jnp.float32)
        s3 = s.reshape(hpg, tq, wk) + bias_w[None]
        p = jnp.exp(s3 - jnp.max(s3, axis=-1, keepdims=True))
        lw = jnp.sum(p, axis=-1, keepdims=True)
        o_w = jnp.dot(p.reshape(rows, wk).astype(jnp.bfloat16), vw,
                      preferred_element_type=jnp.float32) / lw.reshape(rows, 1)

        o = (_gate_col(g_ref, g, 0, hpg) * o_c + _gate_col(g_ref, g, 1, hpg) * o_s
             + _gate_col(g_ref, g, 2, hpg) * o_w)
        for h in range(hpg):
            c0 = (g * hpg + h) * HEAD_DIM
            o_ref[:, c0:c0 + HEAD_DIM] = o[h * tq:(h + 1) * tq, :].astype(o_ref.dtype)


def _attn_prompt(qh, kvh, cmp, gates):
    B, _, T, _ = qh.shape
    tq = min(ATTN_TQ, T)
    nq = T // tq
    ov = _overlap_matrix()
    ex = _expand_matrix(T)
    return pl.pallas_call(
        _attn_prompt_body,
        grid=(B, nq),
        in_specs=[pl.BlockSpec((1, N_HEADS, tq, HEAD_DIM), lambda b, i: (b, 0, i, 0)),
                  pl.BlockSpec((1, N_KV_KINDS * KV_GROUPS, T, HEAD_DIM), lambda b, i: (b, 0, 0, 0)),
                  pl.BlockSpec((1, 2, KV_GROUPS, N_CHUNKS, HEAD_DIM), lambda b, i: (b, 0, 0, 0, 0)),
                  pl.BlockSpec((tq, LANES), lambda b, i: (b * nq + i, 0)),
                  _const_spec(ov.shape), _const_spec(ex.shape)],
        out_specs=pl.BlockSpec((tq, NSA_Q_COLS), lambda b, i: (b * nq + i, 0)),
        out_shape=jax.ShapeDtypeStruct((B * T, NSA_Q_COLS), jnp.bfloat16),
        scratch_shapes=[pltpu.VMEM((T // SEL_KC, tq, SEL_KC), jnp.float32)],
        compiler_params=pltpu.CompilerParams(
            dimension_semantics=("arbitrary", "arbitrary"),
            vmem_limit_bytes=VMEM_LIMIT_BYTES),
        name="nsa_attn_prompt",
    )(qh, kvh, cmp, gates, ov, ex)


def _nsa_prompt_pallas(xp, w_in, cw):
    B, T, D = xp.shape
    _, kv, gates, qh, kvh = _nsa_proj(xp.reshape(B * T, D), w_in, n_seq=B, head_major=True)
    cmp = _cmp_prompt(kv, cw, n_seq=B)
    o = _attn_prompt(qh, kvh, cmp, gates)
    kv6 = kv.reshape(B, T, N_KV_KINDS, KV_GROUPS, HEAD_DIM)
    return o.reshape(B, T, NSA_Q_COLS), kv6[:, :, :4], kv6[:, T - min(WINDOW, T):, 4:6]


_NT = (((1,), (1,)), ((), ()))


def _softmax_pv_t(qg, kts, vts, k_new, v_new, bias):
    f32 = jnp.float32
    s = [jnp.dot(qg, kt, preferred_element_type=f32) for kt in kts]
    s.append(lax.dot_general(qg, k_new, _NT, preferred_element_type=f32))
    s = jnp.concatenate(s, axis=1) + bias
    p = jnp.exp(s - jnp.max(s, axis=-1, keepdims=True))
    l = jnp.sum(p, axis=-1, keepdims=True)
    pb = p.astype(jnp.bfloat16)
    off = 0
    acc = None
    for vt in vts:
        n = vt.shape[1]
        part = lax.dot_general(pb[:, off:off + n], vt, _NT, preferred_element_type=f32)
        acc = part if acc is None else acc + part
        off += n
    acc = acc + jnp.dot(pb[:, off:], v_new, preferred_element_type=f32)
    return acc / l


def _rep_rows(x, n_tok, hpg):
    return jnp.concatenate([jnp.broadcast_to(x[s:s + 1], (hpg, x.shape[1])) for s in range(n_tok)], axis=0)


def _attn_sample_body(pt_ref, *refs, n_pages, n_tok):
    pages = refs[:n_pages]
    (win_ref, q_ref, kvn_ref, g_ref, wp_ref, pe_ref, w1_ref, b1_ref, w2_ref, ov_ref, e_ref,
     o_ref, kc_rows, vc_rows, new_rows, cb_ref) = refs[n_pages:]
    del pt_ref
    hpg = HEADS_PER_GROUP
    page = pages[0].shape[-1]
    past = n_pages * page
    wb = win_ref.shape[-1]
    nk = past + LANES
    nw = wb + LANES
    hd = HEAD_DIM

    @pl.when(pl.program_id(0) == 0)
    def _():
        new_rows[...] = jnp.zeros(new_rows.shape, new_rows.dtype)

    bf = jnp.bfloat16
    for p in range(n_pages):
        r = slice(p * page, (p + 1) * page)
        kc_rows[r, :] = pages[p][0, 0, 0].T
        vc_rows[r, :] = pages[p][0, 0, 1].T
    kvn = kvn_ref[0]
    for i, kind in enumerate((K_SEL, V_SEL, K_WIN, V_WIN)):
        for g in range(KV_GROUPS):
            c0 = (kind * KV_GROUPS + g) * hd
            new_rows[i, g, 0:n_tok, :] = kvn[:, c0:c0 + hd].astype(bf)

    _cmp_bias_init(cb_ref, pe_ref, w1_ref, b1_ref)
    res = []
    for kind, rows_ref in enumerate((kc_rows, vc_rows)):
        res.append(_compress_rows(rows_ref, wp_ref.at[kind], cb_ref[kind, 0:1, :], w2_ref.at[kind]).astype(bf))

    t_col = past + lax.broadcasted_iota(jnp.int32, (SUBLANES, 1), 0)
    cblk = lax.broadcasted_iota(jnp.int32, (1, LANES), 1)
    cmask = ((cblk * CMP_STRIDE + (CMP_BLOCK - 1) <= t_col) & (cblk < N_CHUNKS - 1))[:n_tok]
    kpos = lax.broadcasted_iota(jnp.int32, (1, nk), 1)
    wj = lax.broadcasted_iota(jnp.int32, (1, nw), 1)
    w_pos = past - wb + wj
    dist = t_col - w_pos
    bias_w = jnp.where((dist >= 0) & (dist <= WINDOW) & (w_pos >= 0) & (wj < wb + n_tok), 0.0, NEG_INF)
    bias_w = _rep_rows(bias_w, n_tok, hpg)
    ns = -(-(past + n_tok) // SEL_BLOCK)

    for g in range(KV_GROUPS):
        qg = q_ref[0, g]
        kc = res[0][g * N_CHUNKS:(g + 1) * N_CHUNKS, :]
        vc = res[1][g * N_CHUNKS:(g + 1) * N_CHUNKS, :]
        o_c, imp = _cmp_branch(qg, kc, vc, cmask, ov_ref[...], hpg, n_tok, token_major=True)
        chosen = _block_select(imp, t_col, ns)
        selm = jnp.dot(jnp.where(chosen, 1.0, 0.0).astype(bf), e_ref[...], preferred_element_type=jnp.float32)
        bias_s = _rep_rows(jnp.where((selm > 0.5) & (kpos <= t_col), 0.0, NEG_INF), n_tok, hpg)
        gd = slice(g * hd, (g + 1) * hd)
        o_s = _softmax_pv_t(qg, [pages[p][0, 0, K_SEL, gd, :].astype(bf) for p in range(n_pages)],
                            [pages[p][0, 0, V_SEL, gd, :].astype(bf) for p in range(n_pages)],
                            new_rows[0, g], new_rows[1, g], bias_s)
        o_w = _softmax_pv_t(qg, [win_ref[0, 0, 0, gd, :].astype(bf)], [win_ref[0, 0, 1, gd, :].astype(bf)],
                            new_rows[2, g], new_rows[3, g], bias_w)
        gt = g_ref[0, g]
        o_ref[0, g] = gt[:, 0:1] * o_c + gt[:, 1:2] * o_s + gt[:, 2:3] * o_w


def _attn_sample(pool, layer, page_table, win, q, kvn, gates, cw):
    B, n_pages = page_table.shape
    page = pool.shape[-1]
    n_tok = kvn.shape[1]
    rows = q.shape[2]
    past = n_pages * page
    assert past == N_CHUNKS * CMP_STRIDE and page == LANES
    wb = win.shape[-1]
    ov = _overlap_matrix()
    ex = _expand_matrix(past + LANES)
    consts = (cw['wp'], cw['pe'], cw['w1'], cw['b1'], cw['w2'], ov, ex)
    cspec = lambda a: pl.BlockSpec(a.shape, lambda b, pt, nd=a.ndim: (0,) * nd, pipeline_mode=pl.Buffered(1))
    page_specs = [pl.BlockSpec((1, 1) + pool.shape[2:], lambda b, pt, p=p: (layer, pt[b, p], 0, 0, 0))
                  for p in range(n_pages)]
    grid_spec = pltpu.PrefetchScalarGridSpec(
        num_scalar_prefetch=1,
        grid=(B,),
        in_specs=page_specs + [
            pl.BlockSpec((1, 1) + win.shape[2:], lambda b, pt: (layer, b, 0, 0, 0)),
            pl.BlockSpec((1, KV_GROUPS, rows, HEAD_DIM), lambda b, pt: (b, 0, 0, 0)),
            pl.BlockSpec((1, n_tok, kvn.shape[2]), lambda b, pt: (b, 0, 0)),
            pl.BlockSpec((1, KV_GROUPS, rows, gates.shape[3]), lambda b, pt: (b, 0, 0, 0)),
        ] + [cspec(a) for a in consts],
        out_specs=pl.BlockSpec((1, KV_GROUPS, rows, HEAD_DIM), lambda b, pt: (b, 0, 0, 0)),
        scratch_shapes=[pltpu.VMEM((past, LANES), jnp.float32), pltpu.VMEM((past, LANES), jnp.float32),
                        pltpu.VMEM((4, KV_GROUPS, LANES, HEAD_DIM), jnp.bfloat16),
                        pltpu.VMEM((2, SUBLANES, cw['w2'].shape[1]), jnp.float32)])
    return pl.pallas_call(
        functools.partial(_attn_sample_body, n_pages=n_pages, n_tok=n_tok),
        grid_spec=grid_spec,
        out_shape=jax.ShapeDtypeStruct((B, KV_GROUPS, rows, HEAD_DIM), jnp.float32),
        compiler_params=pltpu.CompilerParams(
            dimension_semantics=("arbitrary",), vmem_limit_bytes=VMEM_LIMIT_BYTES),
        name="nsa_attn_sample",
    )(page_table, *([pool] * n_pages), win, q, kvn, gates, *consts)


def _keys_on_lanes(cache):
    a, n, rows, kinds, g, hd = cache.shape
    return cache.transpose(0, 1, 3, 4, 5, 2).reshape(a, n, kinds, g * hd, rows)


def _nsa_sample_pallas(xs, pool_t, layer, page_table, win_t, win_buf, w_in, cw):
    S, B, D = xs.shape
    hpg = HEADS_PER_GROUP
    q, kv, gates = _nsa_proj(xs.reshape(S * B, D), w_in, n_seq=1, head_major=False)
    q_b = q.reshape(S, B, KV_GROUPS, hpg, HEAD_DIM).transpose(1, 2, 0, 3, 4).reshape(B, KV_GROUPS, S * hpg, HEAD_DIM)
    g_b = gates[:, :GATE_COLS].reshape(S, B, KV_GROUPS, hpg, 3).transpose(1, 2, 0, 3, 4)
    g_b = jnp.pad(g_b.reshape(B, KV_GROUPS, S * hpg, 3), ((0, 0), (0, 0), (0, 0), (0, SUBLANES - 3)))
    kvn = kv.reshape(S, B, NSA_KV_COLS).swapaxes(0, 1)
    o = _attn_sample(pool_t, layer, page_table, win_t, q_b, kvn, g_b, cw)
    o = o.reshape(B, KV_GROUPS, S, hpg, HEAD_DIM).transpose(2, 0, 1, 3, 4).reshape(S, B, NSA_Q_COLS)
    kv6 = kvn.reshape(B, S, N_KV_KINDS, KV_GROUPS, HEAD_DIM)
    win_new = jnp.concatenate([win_buf[:, S:], kv6[:, :, 4:6]], axis=1)
    return o.astype(jnp.bfloat16), kv6[:, :, :4], win_new


def kernel(x_prompt, x_sample, cache_nsa_kv, cache_nsa_win, state_rglru_h, state_rglru_conv,
           state_ffn_conv, page_table, p_prompt, p_sample,
           nsa_w_in, nsa_w_out, nsa_cmp_pe, nsa_cmp_w1, nsa_cmp_b1, nsa_cmp_w2,
           rg_w_in, rg_conv_w, rg_conv_b, rg_gate_a_w, rg_gate_a_b, rg_gate_x_w, rg_gate_x_b,
           rg_lambda, rg_w_out,
           ffn_w_up, ffn_conv_w, ffn_conv_b, ffn_w_down,
           ln_mix_g, ln_mix_b, ln_ffn_g, ln_ffn_b, ple_w_proj, ple_w_gate):
    xp = x_prompt
    xs = x_sample.swapaxes(0, 1)
    pool_t = _keys_on_lanes(cache_nsa_kv)
    win_t = _keys_on_lanes(cache_nsa_win)
    kv_p, kv_s, win_p, win_s = [], [], [], []
    h_p, h_s, rc_p, rc_s = [], [], [], []
    fc_p, fc_s = [], []
    for i in range(DEPTH):
        j = i // N_MIXERS
        if i % N_MIXERS == 0:
            cw = _cmp_weights(nsa_cmp_pe[j], nsa_cmp_w1[j], nsa_cmp_b1[j], nsa_cmp_w2[j])
            o_p, rows_p, wnd_p = _nsa_prompt_pallas(xp, nsa_w_in[j], cw)
            o_s, rows_s, wnd_s = _nsa_sample_pallas(xs, pool_t, j, page_table, win_t, cache_nsa_win[j],
                                                    nsa_w_in[j], cw)
            w_out = nsa_w_out[j]
            kv_p.append(rows_p)
            kv_s.append(rows_s)
            win_p.append(wnd_p)
            win_s.append(wnd_s)
        else:
            rgw = _rg_weights(rg_w_in[j], rg_conv_w[j], rg_conv_b[j], rg_gate_a_w[j], rg_gate_a_b[j],
                              rg_gate_x_w[j], rg_gate_x_b[j], rg_lambda[j])
            o_p, hl_p, cs_p = _rglru_prompt(xp, rgw)
            o_s, hl_s, cs_s = _rglru_sample(xs, state_rglru_h[j], state_rglru_conv[j], rgw)
            w_out = rg_w_out[j]
            h_p.append(hl_p)
            h_s.append(hl_s)
            rc_p.append(cs_p)
            rc_s.append(cs_s)
        w = _tail_weights(w_out, ln_mix_g[i], ln_mix_b[i], ln_ffn_g[i], ln_ffn_b[i], ffn_w_up[i],
                          ffn_conv_w[i], ffn_conv_b[i], ffn_w_down[i], ple_w_proj[i], ple_w_gate[i])
        xp, fs_p = _layer_tail_prompt(xp, o_p, p_prompt[i], w)
        xs, fs_s = _layer_tail_sample(xs, o_s, p_sample[i].swapaxes(0, 1), state_ffn_conv[i], w)
        fc_p.append(fs_p)
        fc_s.append(fs_s)
    return (xp, xs.swapaxes(0, 1),
            jnp.stack(kv_p), jnp.stack(kv_s),
            jnp.stack(win_p), jnp.stack(win_s),
            jnp.stack(h_p), jnp.stack(h_s),
            jnp.stack(rc_p), jnp.stack(rc_s),
            jnp.stack(fc_p), jnp.stack(fc_s))
```

```python
import functools

import jax
import jax.numpy as jnp
import numpy as np
from jax import lax
from jax.experimental import pallas as pl
from jax.experimental.pallas import tpu as pltpu

D_MODEL = 1024
DEPTH = 4
N_MIXERS = 2
N_HEADS = 16
HEAD_DIM = D_MODEL // N_HEADS
KV_GROUPS = 2
HEADS_PER_GROUP = N_HEADS // KV_GROUPS
CMP_BLOCK = 32
CMP_STRIDE = 16
SEL_BLOCK = 64
SEL_TOPK = 16
WINDOW = 512
Q_BLOCK = 64
NSA_Q_COLS = N_HEADS * HEAD_DIM
NSA_KV_COLS = 6 * KV_GROUPS * HEAD_DIM
ATTN_SCALE = HEAD_DIM ** -0.5
FORCE_SCORE = 1e6
NEG_INF = -1e30
D_RNN = D_MODEL
RG_BLOCKS = 4
RG_BLOCK_DIM = D_RNN // RG_BLOCKS
RG_CONV = 4
RG_C = 8.0
D_FF = 3 * D_MODEL
FFN_CONV = 3
ALPHA = (2 * DEPTH) ** 0.25
LN_EPS = 1e-5

VMEM_LIMIT_BYTES = 56 * 1024 * 1024
SUBLANES = 8
FF_CHUNK = 512
TAIL_TM = 512


def _const_spec(shape):
    nd = len(shape)
    return pl.BlockSpec(shape, lambda *_: (0,) * nd, pipeline_mode=pl.Buffered(1))


def _layer_spec(stack, layer):
    nd = stack.ndim
    return pl.BlockSpec((None,) + stack.shape[1:], lambda *_: (layer,) + (0,) * (nd - 1),
                        pipeline_mode=pl.Buffered(1))


def _at_layer(stacks, layer):
    return {k: (v, layer) for k, v in stacks.items()}


def _layer_norm(x, g, b):
    mu = jnp.mean(x, axis=-1, keepdims=True)
    xc = x - mu
    var = jnp.mean(xc * xc, axis=-1, keepdims=True)
    return xc * lax.rsqrt(var + LN_EPS) * g + b


def _tail_body(x_ref, o_ref, p_ref, halo_ref, wout_ref, wup_ref, wdown_ref, wgate_ref, wproj_ref,
               ln_ref, cw_ref, cb_ref, out_ref, st_ref, tmp_ref, acc_ref, *, halo, shift):
    tm = x_ref.shape[0]
    d_ff = wdown_ref.shape[0]

    @pl.when(pl.program_id(1) == 0)
    def _():
        st_ref[...] = halo_ref[...]

    x = x_ref[...]
    mix = jnp.dot(o_ref[...], wout_ref[...], preferred_element_type=jnp.float32)
    h = _layer_norm(ALPHA * x + mix, ln_ref[0:1, :], ln_ref[1:2, :])
    h_bf = h.astype(jnp.bfloat16)

    def conv(cols, half):
        u = jnp.dot(h_bf, wup_ref[:, cols], preferred_element_type=jnp.float32)
        tmp_ref[0:halo, half] = st_ref[:, cols]
        tmp_ref[halo:halo + tm, half] = u
        st_ref[:, cols] = u[tm - halo:tm, :]
        p1 = tmp_ref[halo - shift:halo - shift + tm, half]
        p2 = tmp_ref[halo - 2 * shift:halo - 2 * shift + tm, half]
        return cb_ref[:, cols] + p2 * cw_ref[0:1, cols] + p1 * cw_ref[1:2, cols] + u * cw_ref[2:3, cols]

    for c in range(d_ff // FF_CHUNK):
        gate_u = conv(slice(c * FF_CHUNK, (c + 1) * FF_CHUNK), slice(0, FF_CHUNK))
        value_u = conv(slice(d_ff + c * FF_CHUNK, d_ff + (c + 1) * FF_CHUNK), slice(FF_CHUNK, 2 * FF_CHUNK))
        act = jax.nn.gelu(gate_u) * value_u
        part = jnp.dot(act.astype(jnp.bfloat16), wdown_ref[c * FF_CHUNK:(c + 1) * FF_CHUNK, :],
                       preferred_element_type=jnp.float32)
        if c == 0:
            acc_ref[...] = part
        else:
            acc_ref[...] += part

    y = _layer_norm(ALPHA * h + acc_ref[...], ln_ref[2:3, :], ln_ref[3:4, :])
    gate = jax.nn.sigmoid(jnp.dot(y.astype(jnp.bfloat16), wgate_ref[...],
                                  preferred_element_type=jnp.float32))
    pp = jnp.dot(p_ref[...].astype(jnp.bfloat16), wproj_ref[...],
                 preferred_element_type=jnp.float32)
    out_ref[...] = y + gate * pp


TAIL_WEIGHTS = ('out', 'up', 'down', 'gate', 'proj', 'ln', 'cw', 'cb')


def _tail_call(x, o, p, halo0, w, *, n_seq, tm, halo, shift):
    R, D = x.shape
    T = R // n_seq
    nt = T // tm
    F2 = w['up'][0].shape[-1]
    row = lambda b, t: (b * nt + t, 0)
    seq = lambda b, t: (b, 0)
    return pl.pallas_call(
        functools.partial(_tail_body, halo=halo, shift=shift),
        grid=(n_seq, nt),
        in_specs=[pl.BlockSpec((tm, D), row),
                  pl.BlockSpec((tm, o.shape[1]), row),
                  pl.BlockSpec((tm, p.shape[1]), row),
                  pl.BlockSpec((halo, F2), seq)] + [_layer_spec(*w[k]) for k in TAIL_WEIGHTS],
        out_specs=[pl.BlockSpec((tm, D), row),
                   pl.BlockSpec((halo, F2), seq)],
        out_shape=[jax.ShapeDtypeStruct((R, D), jnp.float32),
                   jax.ShapeDtypeStruct((n_seq * halo, F2), jnp.float32)],
        scratch_shapes=[pltpu.VMEM((halo + tm, 2 * FF_CHUNK), jnp.float32),
                        pltpu.VMEM((tm, D), jnp.float32)],
        compiler_params=pltpu.CompilerParams(
            dimension_semantics=("arbitrary", "arbitrary"),
            vmem_limit_bytes=VMEM_LIMIT_BYTES),
        name="layer_tail",
    )(x, o, p, halo0, *[w[k][0] for k in TAIL_WEIGHTS])


def _tail_weights(w_up, conv_w, conv_b, w_down, ln_mg, ln_mb, ln_fg, ln_fb, w_proj, w_gate):
    bf = jnp.bfloat16
    return dict(up=w_up.astype(bf), down=w_down.astype(bf), gate=w_gate.astype(bf), proj=w_proj.astype(bf),
                ln=jnp.stack([ln_mg, ln_mb, ln_fg, ln_fb], axis=1), cw=conv_w, cb=conv_b[:, None, :])


def _layer_tail_prompt(xp, op, pp, w):
    B, T, D = xp.shape
    F2 = w['up'][0].shape[-1]
    halo0 = jnp.zeros((B * SUBLANES, F2), jnp.float32)
    y, st = _tail_call(xp.reshape(B * T, D), op.reshape(B * T, -1), pp.reshape(B * T, -1), halo0, w,
                       n_seq=B, tm=min(TAIL_TM, T), halo=SUBLANES, shift=1)
    return y.reshape(B, T, D), st.reshape(B, SUBLANES, F2)[:, SUBLANES - (FFN_CONV - 1):, :]


def _layer_tail_sample(xs, os_, ps, state, w):
    S, B, D = xs.shape
    F2 = w['up'][0].shape[-1]
    halo = (FFN_CONV - 1) * B
    halo0 = state.transpose(1, 0, 2).reshape(halo, F2)
    y, st = _tail_call(xs.reshape(S * B, D), os_.reshape(S * B, -1), ps.reshape(S * B, -1), halo0, w,
                       n_seq=1, tm=S * B, halo=halo, shift=B)
    return y.reshape(S, B, D), st.reshape(FFN_CONV - 1, B, F2).transpose(1, 0, 2)


RG_TM = 256


def _rg_body(x_ref, halo_ref, h0_ref, win_ref, cw_ref, cb_ref, gaw_ref, gxw_ref, vec_ref,
             o_ref, st_ref, hl_ref, tmp_ref, *, halo, shift):
    tm = x_ref.shape[0]
    dr = o_ref.shape[1]
    nblk = gaw_ref.shape[0]
    bd = dr // nblk

    @pl.when(pl.program_id(1) == 0)
    def _():
        st_ref[...] = halo_ref[...]
        hl_ref[...] = h0_ref[...]

    u = jnp.dot(x_ref[...].astype(jnp.bfloat16), win_ref[...], preferred_element_type=jnp.float32)
    gate_br = u[:, :dr]
    rec_br = u[:, dr:]
    tmp_ref[0:halo, :] = st_ref[...]
    tmp_ref[halo:halo + tm, :] = rec_br
    st_ref[...] = rec_br[tm - halo:tm, :]
    rec = cb_ref[...] + rec_br * cw_ref[RG_CONV - 1:RG_CONV, :]
    for k in range(1, RG_CONV):
        lo = halo - k * shift
        rec = rec + tmp_ref[lo:lo + tm, :] * cw_ref[RG_CONV - 1 - k:RG_CONV - k, :]

    rec_bf = rec.astype(jnp.bfloat16)
    ga = jnp.concatenate([jnp.dot(rec_bf[:, n * bd:(n + 1) * bd], gaw_ref[n],
                                  preferred_element_type=jnp.float32) for n in range(nblk)], axis=1)
    gx = jnp.concatenate([jnp.dot(rec_bf[:, n * bd:(n + 1) * bd], gxw_ref[n],
                                  preferred_element_type=jnp.float32) for n in range(nblk)], axis=1)
    r = jax.nn.sigmoid(ga + vec_ref[0:1, :])
    i = jax.nn.sigmoid(gx + vec_ref[1:2, :])
    log_a = (-RG_C) * r * jax.nn.softplus(-vec_ref[2:3, :])
    a = jnp.exp(log_a)
    b = jnp.sqrt(-jnp.tanh(log_a) * (a * a + 1.0)) * i * rec

    row = lax.broadcasted_iota(jnp.int32, (tm, 1), 0)
    span = SUBLANES if shift == 1 else tm
    d = shift
    while d < span:
        keep = (row % span) >= d
        a_sh = jnp.where(keep, pltpu.roll(a, d, 0), 1.0)
        b_sh = jnp.where(keep, pltpu.roll(b, d, 0), 0.0)
        b = b + a * b_sh
        a = a * a_sh
        d *= 2
    if shift == 1:
        carry = hl_ref[0:1, :]
        groups = []
        for j in range(tm // span):
            hj = b[j * span:(j + 1) * span, :] + a[j * span:(j + 1) * span, :] * carry
            carry = hj[span - 1:span, :]
            groups.append(hj)
        hs = jnp.concatenate(groups, axis=0)
        hl_ref[...] = jnp.broadcast_to(carry, hl_ref.shape)
    else:
        carry = hl_ref[0:shift, :]
        hs = b + a * jnp.tile(carry, (tm // shift, 1))
        hl_ref[...] = hs[tm - shift:tm, :]
    o_ref[...] = (hs * jax.nn.gelu(gate_br)).astype(o_ref.dtype)


def _rg_call(x, halo0, h0, w, *, n_seq, tm, halo, shift):
    R, D = x.shape
    T = R // n_seq
    nt = T // tm
    dr = w['cw'][0].shape[-1]
    hrows = max(shift, SUBLANES)
    row = lambda b, t: (b * nt + t, 0)
    seq = lambda b, t: (b, 0)
    return pl.pallas_call(
        functools.partial(_rg_body, halo=halo, shift=shift),
        grid=(n_seq, nt),
        in_specs=[pl.BlockSpec((tm, D), row),
                  pl.BlockSpec((halo, dr), seq),
                  pl.BlockSpec((hrows, dr), seq)] + [_layer_spec(*w[k]) for k in RG_WEIGHTS],
        out_specs=[pl.BlockSpec((tm, dr), row),
                   pl.BlockSpec((halo, dr), seq),
                   pl.BlockSpec((hrows, dr), seq)],
        out_shape=[jax.ShapeDtypeStruct((R, dr), jnp.bfloat16),
                   jax.ShapeDtypeStruct((n_seq * halo, dr), jnp.float32),
                   jax.ShapeDtypeStruct((n_seq * hrows, dr), jnp.float32)],
        scratch_shapes=[pltpu.VMEM((halo + tm, dr), jnp.float32)],
        compiler_params=pltpu.CompilerParams(
            dimension_semantics=("arbitrary", "arbitrary"),
            vmem_limit_bytes=VMEM_LIMIT_BYTES),
        name="rglru",
    )(x, halo0, h0, *[w[k][0] for k in RG_WEIGHTS])


RG_WEIGHTS = ('in', 'cw', 'cb', 'ga', 'gx', 'vec')


def _rg_weights(w_in, conv_w, conv_b, ga_w, ga_b, gx_w, gx_b, lam):
    bf = jnp.bfloat16
    return dict(**{'in': w_in.astype(bf)}, cw=conv_w, cb=conv_b[:, None, :], ga=ga_w.astype(bf),
                gx=gx_w.astype(bf), vec=jnp.stack([ga_b, gx_b, lam], axis=1))


def _rglru_prompt(xp, w):
    B, T, D = xp.shape
    dr = w['cw'][0].shape[-1]
    o, st, hl = _rg_call(xp.reshape(B * T, D), jnp.zeros((B * SUBLANES, dr), jnp.float32),
                         jnp.zeros((B * SUBLANES, dr), jnp.float32), w,
                         n_seq=B, tm=min(RG_TM, T), halo=SUBLANES, shift=1)
    st = st.reshape(B, SUBLANES, dr)[:, SUBLANES - (RG_CONV - 1):, :]
    return o.reshape(B, T, dr), hl.reshape(B, SUBLANES, dr)[:, 0, :], st


def _rglru_sample(xs, h0, conv_prev, w):
    S, B, D = xs.shape
    dr = w['cw'][0].shape[-1]
    halo = (RG_CONV - 1) * B
    o, st, hl = _rg_call(xs.reshape(S * B, D), conv_prev.swapaxes(0, 1).reshape(halo, dr), h0, w,
                         n_seq=1, tm=S * B, halo=halo, shift=B)
    return o.reshape(S, B, dr), hl, st.reshape(RG_CONV - 1, B, dr).swapaxes(0, 1)


LANES = 128
NSA_PROJ_TM = 512
N_KV_KINDS = 6
GATE_COLS = 3 * N_HEADS


def _nsa_proj_body(x_ref, w_ref, q_ref, kv_ref, g_ref, *rest, head_major):
    proj = jnp.dot(x_ref[...].astype(jnp.bfloat16), w_ref[...], preferred_element_type=jnp.float32)
    q = proj[:, :NSA_Q_COLS] * ATTN_SCALE
    kv = proj[:, NSA_Q_COLS:NSA_Q_COLS + NSA_KV_COLS]
    q_ref[...] = q.astype(q_ref.dtype)
    kv_ref[...] = kv
    g_ref[...] = jax.nn.sigmoid(proj[:, NSA_Q_COLS + NSA_KV_COLS:])
    if head_major:
        qh_ref, kvh_ref = rest
        for h in range(N_HEADS):
            qh_ref[0, h] = q[:, h * HEAD_DIM:(h + 1) * HEAD_DIM].astype(qh_ref.dtype)
        for k in range(N_KV_KINDS * KV_GROUPS):
            kvh_ref[0, k] = kv[:, k * HEAD_DIM:(k + 1) * HEAD_DIM].astype(kvh_ref.dtype)


def _nsa_proj_weights(w_in):
    n_pad = NSA_Q_COLS + NSA_KV_COLS + LANES
    return jnp.pad(w_in, ((0, 0), (0, 0), (0, n_pad - w_in.shape[-1]))).astype(jnp.bfloat16)


def _nsa_proj(x, w, layer, *, n_seq, head_major):
    R, D = x.shape
    T = R // n_seq
    tm = min(NSA_PROJ_TM, T)
    nt = T // tm
    row = lambda b, t: (b * nt + t, 0)
    out_specs = [pl.BlockSpec((tm, NSA_Q_COLS), row), pl.BlockSpec((tm, NSA_KV_COLS), row),
                 pl.BlockSpec((tm, LANES), row)]
    out_shape = [jax.ShapeDtypeStruct((R, NSA_Q_COLS), jnp.bfloat16),
                 jax.ShapeDtypeStruct((R, NSA_KV_COLS), jnp.float32),
                 jax.ShapeDtypeStruct((R, LANES), jnp.float32)]
    if head_major:
        hm = lambda b, t: (b, 0, t, 0)
        out_specs += [pl.BlockSpec((1, N_HEADS, tm, HEAD_DIM), hm),
                      pl.BlockSpec((1, N_KV_KINDS * KV_GROUPS, tm, HEAD_DIM), hm)]
        out_shape += [jax.ShapeDtypeStruct((n_seq, N_HEADS, T, HEAD_DIM), jnp.bfloat16),
                      jax.ShapeDtypeStruct((n_seq, N_KV_KINDS * KV_GROUPS, T, HEAD_DIM), jnp.bfloat16)]
    return pl.pallas_call(
        functools.partial(_nsa_proj_body, head_major=head_major),
        grid=(n_seq, nt),
        in_specs=[pl.BlockSpec((tm, D), row), _layer_spec(w, layer)],
        out_specs=out_specs,
        out_shape=out_shape,
        compiler_params=pltpu.CompilerParams(
            dimension_semantics=("arbitrary", "arbitrary"),
            vmem_limit_bytes=VMEM_LIMIT_BYTES),
        name="nsa_proj",
    )(x, w)


N_CHUNKS = 128
CMP_HALF = CMP_BLOCK // CMP_STRIDE


def _compress_rows(rows_ref, wp_ref, cb, w2_ref):
    n = N_CHUNKS
    hid = w2_ref.shape[0]
    lane = lax.broadcasted_iota(jnp.int32, (n, LANES), 1)
    low = lane < HEAD_DIM
    z = jnp.zeros((KV_GROUPS * n, 2 * hid), jnp.float32)
    for j in range(CMP_STRIDE // 2):
        va = rows_ref[pl.ds(2 * j, n, stride=CMP_STRIDE), :]
        vb = rows_ref[pl.ds(2 * j + 1, n, stride=CMP_STRIDE), :]
        g0 = jnp.where(low, va, pltpu.roll(vb, HEAD_DIM, 1))
        g1 = jnp.where(low, pltpu.roll(va, HEAD_DIM, 1), vb)
        lhs = jnp.concatenate([g0, g1], axis=0).astype(jnp.bfloat16)
        z = z + jnp.dot(lhs, wp_ref[j], preferred_element_type=jnp.float32)
    za = z[:, :hid]
    zb = pltpu.roll(z[:, hid:], KV_GROUPS * n - 1, 0)
    h = jax.nn.gelu(za + zb + cb)
    return jnp.dot(h.astype(jnp.bfloat16), w2_ref[...], preferred_element_type=jnp.float32)


def _cmp_bias_init(cb_ref, pe_ref, w1_ref, b1_ref):
    @pl.when(pl.program_id(0) == 0)
    def _():
        for kind in range(2):
            cb = b1_ref[kind] + jnp.sum(pe_ref[kind] * w1_ref[kind], axis=0, keepdims=True)
            cb_ref[kind] = jnp.broadcast_to(cb, cb_ref.shape[1:])


CMP_WEIGHTS = ('wp', 'pe', 'w1', 'b1', 'w2')


def _cmp_weights(pe, w1, b1, w2):
    a, kinds, L, hd, hid = w1.shape
    half = L // 2
    wa = w1[:, :, :half].reshape(a, kinds, half // 2, 2 * hd, hid)
    wb = w1[:, :, half:].reshape(a, kinds, half // 2, 2 * hd, hid)
    return dict(wp=jnp.concatenate([wa, wb], axis=-1).astype(jnp.bfloat16),
                pe=pe.reshape(a, kinds, L * hd, 1), w1=w1.reshape(a, kinds, L * hd, hid),
                b1=b1.reshape(a, kinds, 1, hid), w2=w2.astype(jnp.bfloat16))


def _cmp_prompt_body(krows_ref, vrows_ref, wp_ref, pe_ref, w1_ref, b1_ref, w2_ref, out_ref, cb_ref):
    _cmp_bias_init(cb_ref, pe_ref, w1_ref, b1_ref)
    for kind, rows_ref in enumerate((krows_ref, vrows_ref)):
        res = _compress_rows(rows_ref, wp_ref.at[kind], cb_ref[kind, 0:1, :], w2_ref.at[kind])
        for g in range(KV_GROUPS):
            out_ref[0, kind, g] = res[g * N_CHUNKS:(g + 1) * N_CHUNKS, :].astype(out_ref.dtype)


def _cmp_prompt(kv, cw, *, n_seq):
    R = kv.shape[0]
    T = R // n_seq
    assert T == N_CHUNKS * CMP_STRIDE
    return pl.pallas_call(
        _cmp_prompt_body,
        grid=(n_seq,),
        in_specs=[pl.BlockSpec((T, LANES), lambda b: (b, 0)), pl.BlockSpec((T, LANES), lambda b: (b, 1))]
        + [_layer_spec(*cw[k]) for k in CMP_WEIGHTS],
        out_specs=pl.BlockSpec((1, 2, KV_GROUPS, N_CHUNKS, HEAD_DIM), lambda b: (b, 0, 0, 0, 0)),
        out_shape=jax.ShapeDtypeStruct((n_seq, 2, KV_GROUPS, N_CHUNKS, HEAD_DIM), jnp.bfloat16),
        scratch_shapes=[pltpu.VMEM((2, SUBLANES, cw['w2'][0].shape[-2]), jnp.float32)],
        compiler_params=pltpu.CompilerParams(
            dimension_semantics=("arbitrary",), vmem_limit_bytes=VMEM_LIMIT_BYTES),
        name="nsa_compress",
    )(kv, kv, *[cw[k][0] for k in CMP_WEIGHTS])


ATTN_TQ = 128
SEL_KC = 512
K_SEL, V_SEL, K_WIN, V_WIN = 2, 3, 4, 5


def _split3(x):
    hi = x.astype(jnp.bfloat16)
    r = x - hi.astype(jnp.float32)
    mid = r.astype(jnp.bfloat16)
    lo = (r - mid.astype(jnp.float32)).astype(jnp.bfloat16)
    return hi, mid, lo


def _overlap_matrix():
    c = np.arange(LANES)[:, None]
    n = np.arange(LANES)[None, :]
    ov = ((c * CMP_STRIDE < n * SEL_BLOCK + SEL_BLOCK) & (c * CMP_STRIDE + CMP_BLOCK - 1 >= n * SEL_BLOCK)
          & (c < N_CHUNKS - 1))
    return jnp.asarray(ov, jnp.bfloat16)


def _expand_matrix(n_keys):
    return jnp.asarray(np.arange(LANES)[:, None] == (np.arange(n_keys) // SEL_BLOCK)[None, :], jnp.bfloat16)


def _cmp_branch(qg, kc, vc, cmask, ov, hpg, tq, token_major=False):
    rows = hpg * tq
    s = lax.dot_general(qg, kc, (((1,), (1,)), ((), ())), preferred_element_type=jnp.float32)
    if token_major:
        shape3, cm3, head_axis = (tq, hpg, LANES), cmask[:, None, :], 1
    else:
        shape3, cm3, head_axis = (hpg, tq, LANES), cmask[None], 0
    s3 = jnp.where(cm3, s.reshape(shape3), NEG_INF)
    m = jnp.max(s3, axis=-1, keepdims=True)
    e = jnp.where(cm3, jnp.exp(s3 - m), 0.0)
    p3 = e / jnp.maximum(jnp.sum(e, axis=-1, keepdims=True), 1e-30)
    o_c = jnp.dot(p3.reshape(rows, LANES).astype(jnp.bfloat16), vc, preferred_element_type=jnp.float32)
    psum = jnp.sum(p3, axis=head_axis)
    if tq < SUBLANES:
        psum = jnp.concatenate([psum, jnp.zeros((SUBLANES - tq, LANES), jnp.float32)], axis=0)
    hi, mid, lo = _split3(psum)
    imp = (jnp.dot(hi, ov, preferred_element_type=jnp.float32)
           + jnp.dot(mid, ov, preferred_element_type=jnp.float32)
           + jnp.dot(lo, ov, preferred_element_type=jnp.float32))
    return o_c, imp


def _block_select(imp, t_col, ns):
    blk = lax.broadcasted_iota(jnp.int32, imp.shape, 1)
    cur = t_col // SEL_BLOCK
    imp = jnp.where((blk == 0) | (blk == cur) | (blk == cur - 1), FORCE_SCORE, imp)
    imp = jnp.where(blk > cur, -1.0, imp)
    imp = jnp.where(blk >= ns, -2.0, imp)
    rank = jnp.zeros(imp.shape, jnp.int32)
    for mblk in range(ns):
        col = imp[:, mblk:mblk + 1]
        beats = (col > imp) | ((col == imp) & (blk > mblk))
        rank = rank + beats.astype(jnp.int32)
    return rank < min(SEL_TOPK, ns)


def _block_select_t(imp, t_row, ns):
    ns_pad = -(-ns // SUBLANES) * SUBLANES
    it = imp.T[:ns_pad]
    blk = lax.broadcasted_iota(jnp.int32, it.shape, 0)
    cur = t_row // SEL_BLOCK
    it = jnp.where((blk == 0) | (blk == cur) | (blk == cur - 1), FORCE_SCORE, it)
    it = jnp.where(blk > cur, -1.0, it)
    it = jnp.where(blk >= ns, -2.0, it)
    rank = jnp.zeros(it.shape, jnp.int32)
    for mblk in range(ns):
        row = it[mblk:mblk + 1, :]
        beats = (row > it) | ((row == it) & (blk > mblk))
        rank = rank + beats.astype(jnp.int32)
    chosen_t = jnp.where(rank < min(SEL_TOPK, ns), 1.0, 0.0)
    chosen_t = jnp.concatenate([chosen_t, jnp.zeros((LANES - ns_pad, LANES), jnp.float32)], axis=0)
    return chosen_t.T


def _gate_col(g_ref, g, j, hpg):
    cols = [(g * hpg + h) * 3 + j for h in range(hpg)]
    return jnp.concatenate([g_ref[:, c:c + 1] for c in cols], axis=0)


def _attn_prompt_body(q_ref, kvh_ref, cmp_ref, g_ref, ov_ref, e_ref, o_ref, bias_ref):
    tq = q_ref.shape[2]
    T = kvh_ref.shape[2]
    hpg = HEADS_PER_GROUP
    rows = hpg * tq
    wk = WINDOW + tq
    t0 = pl.program_id(1) * tq
    t_col = t0 + lax.broadcasted_iota(jnp.int32, (tq, 1), 0)

    ws = pl.multiple_of(jnp.maximum(t0 - WINDOW, 0), tq)
    dist = t_col - (ws + lax.broadcasted_iota(jnp.int32, (1, wk), 1))
    bias_w = jnp.where((dist >= 0) & (dist <= WINDOW), 0.0, NEG_INF)
    cblk = lax.broadcasted_iota(jnp.int32, (1, LANES), 1)
    cmask = (cblk * CMP_STRIDE + (CMP_BLOCK - 1) <= t_col) & (cblk < N_CHUNKS - 1)
    kpos = lax.broadcasted_iota(jnp.int32, (1, T), 1)
    n_kc = (t0 + tq + SEL_KC - 1) // SEL_KC
    ov = ov_ref[...]
    ns = T // SEL_BLOCK
    t_row = t0 + lax.broadcasted_iota(jnp.int32, (1, tq), 1)
    causal_blocks = jnp.where(lax.broadcasted_iota(jnp.int32, (tq, LANES), 1) <= t_col // SEL_BLOCK, 1.0, 0.0)

    for g in range(KV_GROUPS):
        qg = q_ref[0, g * hpg:(g + 1) * hpg].reshape(rows, HEAD_DIM)
        o_c, imp = _cmp_branch(qg, cmp_ref[0, 0, g], cmp_ref[0, 1, g], cmask, ov, hpg, tq)
        if tq == LANES:
            chosen = lax.cond((t0 + tq - 1) // SEL_BLOCK + 1 > min(SEL_TOPK, ns),
                              lambda imp=imp: _block_select_t(imp, t_row, ns),
                              lambda: causal_blocks)
        else:
            chosen = jnp.where(_block_select(imp, t_col, ns), 1.0, 0.0)
        selk = jnp.dot(chosen.astype(jnp.bfloat16), e_ref[...], preferred_element_type=jnp.float32)
        bias = jnp.where((selk > 0.5) & (kpos <= t_col), 0.0, NEG_INF)
        for c in range(T // SEL_KC):
            bias_ref[c] = bias[:, c * SEL_KC:(c + 1) * SEL_KC]

        def chunk(c, carry, g=g, qg=qg):
            m, l, acc = carry
            k0 = pl.multiple_of(c * SEL_KC, SEL_KC)
            k = kvh_ref[0, K_SEL * KV_GROUPS + g, pl.ds(k0, SEL_KC), :]
            v = kvh_ref[0, V_SEL * KV_GROUPS + g, pl.ds(k0, SEL_KC), :]
            s = lax.dot_general(qg, k, (((1,), (1,)), ((), ())), preferred_element_type=jnp.float32)
            s3 = s.reshape(hpg, tq, SEL_KC) + bias_ref[c][None]
            m_new = jnp.maximum(m, jnp.max(s3, axis=-1, keepdims=True))
            alpha = jnp.exp(m - m_new)
            p = jnp.exp(s3 - m_new)
            l = alpha * l + jnp.sum(p, axis=-1, keepdims=True)
            pv = jnp.dot(p.reshape(rows, SEL_KC).astype(jnp.bfloat16), v,
                         preferred_element_type=jnp.float32)
            return m_new, l, alpha.reshape(rows, 1) * acc + pv

        m, l, acc = lax.fori_loop(
            0, n_kc, chunk,
            (jnp.full((hpg, tq, 1), NEG_INF, jnp.float32), jnp.zeros((hpg, tq, 1), jnp.float32),
             jnp.zeros((rows, HEAD_DIM), jnp.float32)))
        o_s = acc / l.reshape(rows, 1)

        kw = kvh_ref[0, K_WIN * KV_GROUPS + g, pl.ds(ws, wk), :]
        vw = kvh_ref[0, V_WIN * KV_GROUPS + g, pl.ds(ws, wk), :]
        s = lax.dot_general(qg, kw, (((1,), (1,)), ((), ())), preferred_element_type=jnp.float32)
        s3 = s.reshape(hpg, tq, wk) + bias_w[None]
        p = jnp.exp(s3 - jnp.max(s3, axis=-1, keepdims=True))
        lw = jnp.sum(p, axis=-1, keepdims=True)
        o_w = jnp.dot(p.reshape(rows, wk).astype(jnp.bfloat16), vw,
                      preferred_element_type=jnp.float32) / lw.reshape(rows, 1)

        o = (_gate_col(g_ref, g, 0, hpg) * o_c + _gate_col(g_ref, g, 1, hpg) * o_s
             + _gate_col(g_ref, g, 2, hpg) * o_w)
        for h in range(hpg):
            c0 = (g * hpg + h) * HEAD_DIM
            o_ref[:, c0:c0 + HEAD_DIM] = o[h * tq:(h + 1) * tq, :].astype(o_ref.dtype)


def _attn_prompt(qh, kvh, cmp, gates):
    B, _, T, _ = qh.shape
    tq = min(ATTN_TQ, T)
    nq = T // tq
    ov = _overlap_matrix()
    ex = _expand_matrix(T)
    return pl.pallas_call(
        _attn_prompt_body,
        grid=(B, nq),
        in_specs=[pl.BlockSpec((1, N_HEADS, tq, HEAD_DIM), lambda b, i: (b, 0, i, 0)),
                  pl.BlockSpec((1, N_KV_KINDS * KV_GROUPS, T, HEAD_DIM), lambda b, i: (b, 0, 0, 0)),
                  pl.BlockSpec((1, 2, KV_GROUPS, N_CHUNKS, HEAD_DIM), lambda b, i: (b, 0, 0, 0, 0)),
                  pl.BlockSpec((tq, LANES), lambda b, i: (b * nq + i, 0)),
                  _const_spec(ov.shape), _const_spec(ex.shape)],
        out_specs=pl.BlockSpec((tq, NSA_Q_COLS), lambda b, i: (b * nq + i, 0)),
        out_shape=jax.ShapeDtypeStruct((B * T, NSA_Q_COLS), jnp.bfloat16),
        scratch_shapes=[pltpu.VMEM((T // SEL_KC, tq, SEL_KC), jnp.float32)],
        compiler_params=pltpu.CompilerParams(
            dimension_semantics=("arbitrary", "arbitrary"),
            vmem_limit_bytes=VMEM_LIMIT_BYTES),
        name="nsa_attn_prompt",
    )(qh, kvh, cmp, gates, ov, ex)


def _nsa_prompt_pallas(xp, w_in, cw, layer):
    B, T, D = xp.shape
    _, kv, gates, qh, kvh = _nsa_proj(xp.reshape(B * T, D), w_in, layer, n_seq=B, head_major=True)
    cmp = _cmp_prompt(kv, _at_layer(cw, layer), n_seq=B)
    o = _attn_prompt(qh, kvh, cmp, gates)
    kv6 = kv.reshape(B, T, N_KV_KINDS, KV_GROUPS, HEAD_DIM)
    return o.reshape(B, T, NSA_Q_COLS), kv6[:, :, :4], kv6[:, T - min(WINDOW, T):, 4:6]


_NT = (((1,), (1,)), ((), ()))


def _softmax_pv_t(qg, kts, vts, k_new, v_new, bias):
    f32 = jnp.float32
    s = [jnp.dot(qg, kt, preferred_element_type=f32) for kt in kts]
    s.append(lax.dot_general(qg, k_new, _NT, preferred_element_type=f32))
    s = jnp.concatenate(s, axis=1) + bias
    p = jnp.exp(s - jnp.max(s, axis=-1, keepdims=True))
    l = jnp.sum(p, axis=-1, keepdims=True)
    pb = p.astype(jnp.bfloat16)
    off = 0
    acc = None
    for vt in vts:
        n = vt.shape[1]
        part = lax.dot_general(pb[:, off:off + n], vt, _NT, preferred_element_type=f32)
        acc = part if acc is None else acc + part
        off += n
    acc = acc + jnp.dot(pb[:, off:], v_new, preferred_element_type=f32)
    return acc / l


def _rep_rows(x, n_tok, hpg):
    return jnp.concatenate([jnp.broadcast_to(x[s:s + 1], (hpg, x.shape[1])) for s in range(n_tok)], axis=0)


def _attn_sample_body(pt_ref, *refs, n_pages, n_tok):
    pages = refs[:n_pages]
    (win_ref, q_ref, kvn_ref, g_ref, wp_ref, pe_ref, w1_ref, b1_ref, w2_ref, ov_ref, e_ref,
     o_ref, kc_rows, vc_rows, new_rows, cb_ref) = refs[n_pages:]
    del pt_ref
    hpg = HEADS_PER_GROUP
    page = pages[0].shape[-1]
    past = n_pages * page
    wb = win_ref.shape[-1]
    nk = past + LANES
    nw = wb + LANES
    hd = HEAD_DIM

    @pl.when(pl.program_id(0) == 0)
    def _():
        new_rows[...] = jnp.zeros(new_rows.shape, new_rows.dtype)

    bf = jnp.bfloat16
    for p in range(n_pages):
        r = slice(p * page, (p + 1) * page)
        kc_rows[r, :] = pages[p][0, 0, 0].T
        vc_rows[r, :] = pages[p][0, 0, 1].T
    kvn = kvn_ref[0]
    for i, kind in enumerate((K_SEL, V_SEL, K_WIN, V_WIN)):
        for g in range(KV_GROUPS):
            c0 = (kind * KV_GROUPS + g) * hd
            new_rows[i, g, 0:n_tok, :] = kvn[:, c0:c0 + hd].astype(bf)

    _cmp_bias_init(cb_ref, pe_ref, w1_ref, b1_ref)
    res = []
    for kind, rows_ref in enumerate((kc_rows, vc_rows)):
        res.append(_compress_rows(rows_ref, wp_ref.at[kind], cb_ref[kind, 0:1, :], w2_ref.at[kind]).astype(bf))

    t_col = past + lax.broadcasted_iota(jnp.int32, (SUBLANES, 1), 0)
    cblk = lax.broadcasted_iota(jnp.int32, (1, LANES), 1)
    cmask = ((cblk * CMP_STRIDE + (CMP_BLOCK - 1) <= t_col) & (cblk < N_CHUNKS - 1))[:n_tok]
    kpos = lax.broadcasted_iota(jnp.int32, (1, nk), 1)
    wj = lax.broadcasted_iota(jnp.int32, (1, nw), 1)
    w_pos = past - wb + wj
    dist = t_col - w_pos
    bias_w = jnp.where((dist >= 0) & (dist <= WINDOW) & (w_pos >= 0) & (wj < wb + n_tok), 0.0, NEG_INF)
    bias_w = _rep_rows(bias_w, n_tok, hpg)
    ns = -(-(past + n_tok) // SEL_BLOCK)

    for g in range(KV_GROUPS):
        qg = q_ref[0, g]
        kc = res[0][g * N_CHUNKS:(g + 1) * N_CHUNKS, :]
        vc = res[1][g * N_CHUNKS:(g + 1) * N_CHUNKS, :]
        o_c, imp = _cmp_branch(qg, kc, vc, cmask, ov_ref[...], hpg, n_tok, token_major=True)
        chosen = _block_select(imp, t_col, ns)
        selm = jnp.dot(jnp.where(chosen, 1.0, 0.0).astype(bf), e_ref[...], preferred_element_type=jnp.float32)
        bias_s = _rep_rows(jnp.where((selm > 0.5) & (kpos <= t_col), 0.0, NEG_INF), n_tok, hpg)
        gd = slice(g * hd, (g + 1) * hd)
        o_s = _softmax_pv_t(qg, [pages[p][0, 0, K_SEL, gd, :].astype(bf) for p in range(n_pages)],
                            [pages[p][0, 0, V_SEL, gd, :].astype(bf) for p in range(n_pages)],
                            new_rows[0, g], new_rows[1, g], bias_s)
        o_w = _softmax_pv_t(qg, [win_ref[0, 0, 0, gd, :].astype(bf)], [win_ref[0, 0, 1, gd, :].astype(bf)],
                            new_rows[2, g], new_rows[3, g], bias_w)
        gt = g_ref[0, g]
        o_ref[0, g] = gt[:, 0:1] * o_c + gt[:, 1:2] * o_s + gt[:, 2:3] * o_w


def _attn_sample(pool, layer, page_table, win, q, kvn, gates, cw):
    B, n_pages = page_table.shape
    page = pool.shape[-1]
    n_tok = kvn.shape[1]
    rows = q.shape[2]
    past = n_pages * page
    assert past == N_CHUNKS * CMP_STRIDE and page == LANES
    wb = win.shape[-1]
    ov = _overlap_matrix()
    ex = _expand_matrix(past + LANES)
    cwl = _at_layer(cw, layer)
    page_specs = [pl.BlockSpec((1, 1) + pool.shape[2:], lambda b, pt, p=p: (layer, pt[b, p], 0, 0, 0))
                  for p in range(n_pages)]
    grid_spec = pltpu.PrefetchScalarGridSpec(
        num_scalar_prefetch=1,
        grid=(B,),
        in_specs=page_specs + [
            pl.BlockSpec((1, 1) + win.shape[2:], lambda b, pt: (layer, b, 0, 0, 0)),
            pl.BlockSpec((1, KV_GROUPS, rows, HEAD_DIM), lambda b, pt: (b, 0, 0, 0)),
            pl.BlockSpec((1, n_tok, kvn.shape[2]), lambda b, pt: (b, 0, 0)),
            pl.BlockSpec((1, KV_GROUPS, rows, gates.shape[3]), lambda b, pt: (b, 0, 0, 0)),
        ] + [_layer_spec(*cwl[k]) for k in CMP_WEIGHTS] + [_const_spec(ov.shape), _const_spec(ex.shape)],
        out_specs=pl.BlockSpec((1, KV_GROUPS, rows, HEAD_DIM), lambda b, pt: (b, 0, 0, 0)),
        scratch_shapes=[pltpu.VMEM((past, LANES), jnp.float32), pltpu.VMEM((past, LANES), jnp.float32),
                        pltpu.VMEM((4, KV_GROUPS, LANES, HEAD_DIM), jnp.bfloat16),
                        pltpu.VMEM((2, SUBLANES, cw['w2'].shape[-2]), jnp.float32)])
    return pl.pallas_call(
        functools.partial(_attn_sample_body, n_pages=n_pages, n_tok=n_tok),
        grid_spec=grid_spec,
        out_shape=jax.ShapeDtypeStruct((B, KV_GROUPS, rows, HEAD_DIM), jnp.float32),
        compiler_params=pltpu.CompilerParams(
            dimension_semantics=("arbitrary",), vmem_limit_bytes=VMEM_LIMIT_BYTES),
        name="nsa_attn_sample",
    )(page_table, *([pool] * n_pages), win, q, kvn, gates, *[cw[k] for k in CMP_WEIGHTS], ov, ex)


def _keys_on_lanes(cache):
    a, n, rows, kinds, g, hd = cache.shape
    return cache.transpose(0, 1, 3, 4, 5, 2).reshape(a, n, kinds, g * hd, rows)


def _nsa_sample_pallas(xs, pool_t, page_table, win_t, w_in, cw, layer):
    S, B, D = xs.shape
    hpg = HEADS_PER_GROUP
    q, kv, gates = _nsa_proj(xs.reshape(S * B, D), w_in, layer, n_seq=1, head_major=False)
    q_b = q.reshape(S, B, KV_GROUPS, hpg, HEAD_DIM).transpose(1, 2, 0, 3, 4).reshape(B, KV_GROUPS, S * hpg, HEAD_DIM)
    g_b = gates[:, :GATE_COLS].reshape(S, B, KV_GROUPS, hpg, 3).transpose(1, 2, 0, 3, 4)
    g_b = jnp.pad(g_b.reshape(B, KV_GROUPS, S * hpg, 3), ((0, 0), (0, 0), (0, 0), (0, SUBLANES - 3)))
    kvn = kv.reshape(S, B, NSA_KV_COLS).swapaxes(0, 1)
    o = _attn_sample(pool_t, layer, page_table, win_t, q_b, kvn, g_b, cw)
    o = o.reshape(B, KV_GROUPS, S, hpg, HEAD_DIM).transpose(2, 0, 1, 3, 4).reshape(S, B, NSA_Q_COLS)
    kv6 = kvn.reshape(B, S, N_KV_KINDS, KV_GROUPS, HEAD_DIM)
    return o.astype(jnp.bfloat16), kv6[:, :, :4], kv6[:, :, 4:6]


def kernel(x_prompt, x_sample, cache_nsa_kv, cache_nsa_win, state_rglru_h, state_rglru_conv,
           state_ffn_conv, page_table, p_prompt, p_sample,
           nsa_w_in, nsa_w_out, nsa_cmp_pe, nsa_cmp_w1, nsa_cmp_b1, nsa_cmp_w2,
           rg_w_in, rg_conv_w, rg_conv_b, rg_gate_a_w, rg_gate_a_b, rg_gate_x_w, rg_gate_x_b,
           rg_lambda, rg_w_out,
           ffn_w_up, ffn_conv_w, ffn_conv_b, ffn_w_down,
           ln_mix_g, ln_mix_b, ln_ffn_g, ln_ffn_b, ple_w_proj, ple_w_gate):
    xp = x_prompt
    xs = x_sample.swapaxes(0, 1)
    n_tok = x_sample.shape[1]
    pool_t = _keys_on_lanes(cache_nsa_kv)
    win_t = _keys_on_lanes(cache_nsa_win)
    bf = jnp.bfloat16
    nsa_in = _nsa_proj_weights(nsa_w_in)
    cmp_w = _cmp_weights(nsa_cmp_pe, nsa_cmp_w1, nsa_cmp_b1, nsa_cmp_w2)
    rg_w = _rg_weights(rg_w_in, rg_conv_w, rg_conv_b, rg_gate_a_w, rg_gate_a_b, rg_gate_x_w, rg_gate_x_b,
                       rg_lambda)
    tail_w = _tail_weights(ffn_w_up, ffn_conv_w, ffn_conv_b, ffn_w_down, ln_mix_g, ln_mix_b, ln_ffn_g,
                           ln_ffn_b, ple_w_proj, ple_w_gate)
    mixer_out = (nsa_w_out.astype(bf), rg_w_out.astype(bf))
    p_sample_t = p_sample.swapaxes(1, 2)
    kv_p, kv_s, win_p, win_s = [], [], [], []
    h_p, h_s, rc_p, rc_s = [], [], [], []
    fc_p, fc_s = [], []
    for i in range(DEPTH):
        j = i // N_MIXERS
        if i % N_MIXERS == 0:
            o_p, rows_p, wnd_p = _nsa_prompt_pallas(xp, nsa_in, cmp_w, j)
            o_s, rows_s, wnd_s = _nsa_sample_pallas(xs, pool_t, page_table, win_t, nsa_in, cmp_w, j)
            kv_p.append(rows_p)
            kv_s.append(rows_s)
            win_p.append(wnd_p)
            win_s.append(wnd_s)
        else:
            rgw = _at_layer(rg_w, j)
            o_p, hl_p, cs_p = _rglru_prompt(xp, rgw)
            o_s, hl_s, cs_s = _rglru_sample(xs, state_rglru_h[j], state_rglru_conv[j], rgw)
            h_p.append(hl_p)
            h_s.append(hl_s)
            rc_p.append(cs_p)
            rc_s.append(cs_s)
        w = dict(_at_layer(tail_w, i), out=(mixer_out[i % N_MIXERS], j))
        xp, fs_p = _layer_tail_prompt(xp, o_p, p_prompt[i], w)
        xs, fs_s = _layer_tail_sample(xs, o_s, p_sample_t[i], state_ffn_conv[i], w)
        fc_p.append(fs_p)
        fc_s.append(fs_s)
    win_sample = jnp.concatenate([cache_nsa_win[:, :, n_tok:], jnp.stack(win_s)], axis=2)
    return (xp, xs.swapaxes(0, 1),
            jnp.stack(kv_p), jnp.stack(kv_s),
            jnp.stack(win_p), win_sample,
            jnp.stack(h_p), jnp.stack(h_s),
            jnp.stack(rc_p), jnp.stack(rc_s),
            jnp.stack(fc_p), jnp.stack(fc_s))
```

```python
import functools

import jax
import jax.numpy as jnp
import numpy as np
from jax import lax
from jax.experimental import pallas as pl
from jax.experimental.pallas import tpu as pltpu

D_MODEL = 1024
DEPTH = 4
N_MIXERS = 2
N_HEADS = 16
HEAD_DIM = D_MODEL // N_HEADS
KV_GROUPS = 2
HEADS_PER_GROUP = N_HEADS // KV_GROUPS
CMP_BLOCK = 32
CMP_STRIDE = 16
SEL_BLOCK = 64
SEL_TOPK = 16
WINDOW = 512
Q_BLOCK = 64
NSA_Q_COLS = N_HEADS * HEAD_DIM
NSA_KV_COLS = 6 * KV_GROUPS * HEAD_DIM
ATTN_SCALE = HEAD_DIM ** -0.5
FORCE_SCORE = 1e6
NEG_INF = -1e30
D_RNN = D_MODEL
RG_BLOCKS = 4
RG_BLOCK_DIM = D_RNN // RG_BLOCKS
RG_CONV = 4
RG_C = 8.0
D_FF = 3 * D_MODEL
FFN_CONV = 3
ALPHA = (2 * DEPTH) ** 0.25
LN_EPS = 1e-5

VMEM_LIMIT_BYTES = 56 * 1024 * 1024
SUBLANES = 8
FF_CHUNK = 1536
TAIL_TM = 512


def _const_spec(shape):
    nd = len(shape)
    return pl.BlockSpec(shape, lambda *_: (0,) * nd, pipeline_mode=pl.Buffered(1))


def _layer_spec(stack, layer):
    nd = stack.ndim
    return pl.BlockSpec((None,) + stack.shape[1:], lambda *_: (layer,) + (0,) * (nd - 1),
                        pipeline_mode=pl.Buffered(1))


def _at_layer(stacks, layer):
    return {k: (v, layer) for k, v in stacks.items()}


def _layer_norm(x, g, b):
    mu = jnp.mean(x, axis=-1, keepdims=True)
    xc = x - mu
    var = jnp.mean(xc * xc, axis=-1, keepdims=True)
    return xc * lax.rsqrt(var + LN_EPS) * g + b


def _tail_body(x_ref, o_ref, p_ref, halo_ref, wout_ref, wup_ref, wdown_ref, wgate_ref, wproj_ref,
               ln_ref, cw_ref, cb_ref, out_ref, st_ref, tmp_ref, acc_ref, *, halo, shift):
    tm = x_ref.shape[0]
    d_ff = wdown_ref.shape[0]

    @pl.when(pl.program_id(1) == 0)
    def _():
        st_ref[...] = halo_ref[...]

    x = x_ref[...]
    mix = jnp.dot(o_ref[...], wout_ref[...], preferred_element_type=jnp.float32)
    h = _layer_norm(ALPHA * x + mix, ln_ref[0:1, :], ln_ref[1:2, :])
    h_bf = h.astype(jnp.bfloat16)

    def conv(cols, half):
        u = jnp.dot(h_bf, wup_ref[:, cols], preferred_element_type=jnp.float32)
        tmp_ref[0:halo, half] = st_ref[:, cols]
        tmp_ref[halo:halo + tm, half] = u
        st_ref[:, cols] = u[tm - halo:tm, :]
        p1 = tmp_ref[halo - shift:halo - shift + tm, half]
        p2 = tmp_ref[halo - 2 * shift:halo - 2 * shift + tm, half]
        return cb_ref[:, cols] + p2 * cw_ref[0:1, cols] + p1 * cw_ref[1:2, cols] + u * cw_ref[2:3, cols]

    for c in range(d_ff // FF_CHUNK):
        gate_u = conv(slice(c * FF_CHUNK, (c + 1) * FF_CHUNK), slice(0, FF_CHUNK))
        value_u = conv(slice(d_ff + c * FF_CHUNK, d_ff + (c + 1) * FF_CHUNK), slice(FF_CHUNK, 2 * FF_CHUNK))
        act = jax.nn.gelu(gate_u) * value_u
        part = jnp.dot(act.astype(jnp.bfloat16), wdown_ref[c * FF_CHUNK:(c + 1) * FF_CHUNK, :],
                       preferred_element_type=jnp.float32)
        if c == 0:
            acc_ref[...] = part
        else:
            acc_ref[...] += part

    y = _layer_norm(ALPHA * h + acc_ref[...], ln_ref[2:3, :], ln_ref[3:4, :])
    gate = jax.nn.sigmoid(jnp.dot(y.astype(jnp.bfloat16), wgate_ref[...],
                                  preferred_element_type=jnp.float32))
    pp = jnp.dot(p_ref[...].astype(jnp.bfloat16), wproj_ref[...],
                 preferred_element_type=jnp.float32)
    out_ref[...] = y + gate * pp


TAIL_WEIGHTS = ('out', 'up', 'down', 'gate', 'proj', 'ln', 'cw', 'cb')


def _tail_call(x, o, p, halo0, w, *, n_seq, tm, halo, shift):
    R, D = x.shape
    T = R // n_seq
    nt = T // tm
    F2 = w['up'][0].shape[-1]
    assert F2 % (2 * FF_CHUNK) == 0 and T % tm == 0, (F2, FF_CHUNK, T, tm)
    row = lambda b, t: (b * nt + t, 0)
    seq = lambda b, t: (b, 0)
    return pl.pallas_call(
        functools.partial(_tail_body, halo=halo, shift=shift),
        grid=(n_seq, nt),
        in_specs=[pl.BlockSpec((tm, D), row),
                  pl.BlockSpec((tm, o.shape[1]), row),
                  pl.BlockSpec((tm, p.shape[1]), row),
                  pl.BlockSpec((halo, F2), seq)] + [_layer_spec(*w[k]) for k in TAIL_WEIGHTS],
        out_specs=[pl.BlockSpec((tm, D), row),
                   pl.BlockSpec((halo, F2), seq)],
        out_shape=[jax.ShapeDtypeStruct((R, D), jnp.float32),
                   jax.ShapeDtypeStruct((n_seq * halo, F2), jnp.float32)],
        scratch_shapes=[pltpu.VMEM((halo + tm, 2 * FF_CHUNK), jnp.float32),
                        pltpu.VMEM((tm, D), jnp.float32)],
        compiler_params=pltpu.CompilerParams(
            dimension_semantics=("arbitrary", "arbitrary"),
            vmem_limit_bytes=VMEM_LIMIT_BYTES),
        name="layer_tail",
    )(x, o, p, halo0, *[w[k][0] for k in TAIL_WEIGHTS])


def _tail_weights(w_up, conv_w, conv_b, w_down, ln_mg, ln_mb, ln_fg, ln_fb, w_proj, w_gate):
    bf = jnp.bfloat16
    return dict(up=w_up.astype(bf), down=w_down.astype(bf), gate=w_gate.astype(bf), proj=w_proj.astype(bf),
                ln=jnp.stack([ln_mg, ln_mb, ln_fg, ln_fb], axis=1), cw=conv_w, cb=conv_b[:, None, :])


def _layer_tail_prompt(xp, op, pp, w):
    B, T, D = xp.shape
    F2 = w['up'][0].shape[-1]
    halo0 = jnp.zeros((B * SUBLANES, F2), jnp.float32)
    y, st = _tail_call(xp.reshape(B * T, D), op.reshape(B * T, -1), pp.reshape(B * T, -1), halo0, w,
                       n_seq=B, tm=min(TAIL_TM, T), halo=SUBLANES, shift=1)
    return y.reshape(B, T, D), st.reshape(B, SUBLANES, F2)[:, SUBLANES - (FFN_CONV - 1):, :]


def _layer_tail_sample(xs, os_, ps, state, w):
    S, B, D = xs.shape
    F2 = w['up'][0].shape[-1]
    halo = (FFN_CONV - 1) * B
    halo0 = state.transpose(1, 0, 2).reshape(halo, F2)
    y, st = _tail_call(xs.reshape(S * B, D), os_.reshape(S * B, -1), ps.reshape(S * B, -1), halo0, w,
                       n_seq=1, tm=S * B, halo=halo, shift=B)
    return y.reshape(S, B, D), st.reshape(FFN_CONV - 1, B, F2).transpose(1, 0, 2)


RG_TM = 512


def _rg_body(x_ref, halo_ref, h0_ref, win_ref, cw_ref, cb_ref, gaw_ref, gxw_ref, vec_ref,
             o_ref, st_ref, hl_ref, tmp_ref, *, halo, shift):
    tm = x_ref.shape[0]
    dr = o_ref.shape[1]
    nblk = gaw_ref.shape[0]
    bd = dr // nblk

    @pl.when(pl.program_id(1) == 0)
    def _():
        st_ref[...] = halo_ref[...]
        hl_ref[...] = h0_ref[...]

    u = jnp.dot(x_ref[...].astype(jnp.bfloat16), win_ref[...], preferred_element_type=jnp.float32)
    gate_br = u[:, :dr]
    rec_br = u[:, dr:]
    tmp_ref[0:halo, :] = st_ref[...]
    tmp_ref[halo:halo + tm, :] = rec_br
    st_ref[...] = rec_br[tm - halo:tm, :]
    rec = cb_ref[...] + rec_br * cw_ref[RG_CONV - 1:RG_CONV, :]
    for k in range(1, RG_CONV):
        lo = halo - k * shift
        rec = rec + tmp_ref[lo:lo + tm, :] * cw_ref[RG_CONV - 1 - k:RG_CONV - k, :]

    rec_bf = rec.astype(jnp.bfloat16)
    ga = jnp.concatenate([jnp.dot(rec_bf[:, n * bd:(n + 1) * bd], gaw_ref[n],
                                  preferred_element_type=jnp.float32) for n in range(nblk)], axis=1)
    gx = jnp.concatenate([jnp.dot(rec_bf[:, n * bd:(n + 1) * bd], gxw_ref[n],
                                  preferred_element_type=jnp.float32) for n in range(nblk)], axis=1)
    r = jax.nn.sigmoid(ga + vec_ref[0:1, :])
    i = jax.nn.sigmoid(gx + vec_ref[1:2, :])
    log_a = (-RG_C) * r * jax.nn.softplus(-vec_ref[2:3, :])
    a = jnp.exp(log_a)
    b = jnp.sqrt(-jnp.tanh(log_a) * (a * a + 1.0)) * i * rec

    if shift == 1:
        span = SUBLANES
        a = a.reshape(tm // span, span, dr)
        b = b.reshape(tm // span, span, dr)
        sub = lax.broadcasted_iota(jnp.int32, (1, span, 1), 1)
        d = 1
        while d < span:
            keep = sub >= d
            a_sh = jnp.where(keep, pltpu.roll(a, d, 1), 1.0)
            b_sh = jnp.where(keep, pltpu.roll(b, d, 1), 0.0)
            b = b + a * b_sh
            a = a * a_sh
            d *= 2
        carry = hl_ref[0:1, :]
        groups = []
        for j in range(tm // span):
            hj = b[j] + a[j] * carry
            carry = hj[span - 1:span, :]
            groups.append(hj)
        hs = jnp.concatenate(groups, axis=0)
        hl_ref[...] = jnp.broadcast_to(carry, hl_ref.shape)
    else:
        row = lax.broadcasted_iota(jnp.int32, (tm, 1), 0)
        d = shift
        while d < tm:
            keep = row >= d
            a_sh = jnp.where(keep, pltpu.roll(a, d, 0), 1.0)
            b_sh = jnp.where(keep, pltpu.roll(b, d, 0), 0.0)
            b = b + a * b_sh
            a = a * a_sh
            d *= 2
        carry = hl_ref[0:shift, :]
        hs = b + a * jnp.tile(carry, (tm // shift, 1))
        hl_ref[...] = hs[tm - shift:tm, :]
    o_ref[...] = (hs * jax.nn.gelu(gate_br)).astype(o_ref.dtype)


def _rg_call(x, halo0, h0, w, *, n_seq, tm, halo, shift):
    R, D = x.shape
    T = R // n_seq
    nt = T // tm
    dr = w['cw'][0].shape[-1]
    hrows = max(shift, SUBLANES)
    row = lambda b, t: (b * nt + t, 0)
    seq = lambda b, t: (b, 0)
    return pl.pallas_call(
        functools.partial(_rg_body, halo=halo, shift=shift),
        grid=(n_seq, nt),
        in_specs=[pl.BlockSpec((tm, D), row),
                  pl.BlockSpec((halo, dr), seq),
                  pl.BlockSpec((hrows, dr), seq)] + [_layer_spec(*w[k]) for k in RG_WEIGHTS],
        out_specs=[pl.BlockSpec((tm, dr), row),
                   pl.BlockSpec((halo, dr), seq),
                   pl.BlockSpec((hrows, dr), seq)],
        out_shape=[jax.ShapeDtypeStruct((R, dr), jnp.bfloat16),
                   jax.ShapeDtypeStruct((n_seq * halo, dr), jnp.float32),
                   jax.ShapeDtypeStruct((n_seq * hrows, dr), jnp.float32)],
        scratch_shapes=[pltpu.VMEM((halo + tm, dr), jnp.float32)],
        compiler_params=pltpu.CompilerParams(
            dimension_semantics=("arbitrary", "arbitrary"),
            vmem_limit_bytes=VMEM_LIMIT_BYTES),
        name="rglru",
    )(x, halo0, h0, *[w[k][0] for k in RG_WEIGHTS])


RG_WEIGHTS = ('in', 'cw', 'cb', 'ga', 'gx', 'vec')


def _rg_weights(w_in, conv_w, conv_b, ga_w, ga_b, gx_w, gx_b, lam):
    bf = jnp.bfloat16
    return dict(**{'in': w_in.astype(bf)}, cw=conv_w, cb=conv_b[:, None, :], ga=ga_w.astype(bf),
                gx=gx_w.astype(bf), vec=jnp.stack([ga_b, gx_b, lam], axis=1))


def _rglru_prompt(xp, w):
    B, T, D = xp.shape
    dr = w['cw'][0].shape[-1]
    o, st, hl = _rg_call(xp.reshape(B * T, D), jnp.zeros((B * SUBLANES, dr), jnp.float32),
                         jnp.zeros((B * SUBLANES, dr), jnp.float32), w,
                         n_seq=B, tm=min(RG_TM, T), halo=SUBLANES, shift=1)
    st = st.reshape(B, SUBLANES, dr)[:, SUBLANES - (RG_CONV - 1):, :]
    return o.reshape(B, T, dr), hl.reshape(B, SUBLANES, dr)[:, 0, :], st


def _rglru_sample(xs, h0, conv_prev, w):
    S, B, D = xs.shape
    dr = w['cw'][0].shape[-1]
    halo = (RG_CONV - 1) * B
    o, st, hl = _rg_call(xs.reshape(S * B, D), conv_prev.swapaxes(0, 1).reshape(halo, dr), h0, w,
                         n_seq=1, tm=S * B, halo=halo, shift=B)
    return o.reshape(S, B, dr), hl, st.reshape(RG_CONV - 1, B, dr).swapaxes(0, 1)


LANES = 128
NSA_PROJ_TM = 512
N_KV_KINDS = 6
GATE_COLS = 3 * N_HEADS


def _nsa_proj_body(x_ref, w_ref, q_ref, kv_ref, g_ref, *rest, head_major):
    proj = jnp.dot(x_ref[...].astype(jnp.bfloat16), w_ref[...], preferred_element_type=jnp.float32)
    q = proj[:, :NSA_Q_COLS] * ATTN_SCALE
    kv = proj[:, NSA_Q_COLS:NSA_Q_COLS + NSA_KV_COLS]
    q_ref[...] = q.astype(q_ref.dtype)
    kv_ref[...] = kv
    g_ref[...] = jax.nn.sigmoid(proj[:, NSA_Q_COLS + NSA_KV_COLS:])
    if head_major:
        qh_ref, kvh_ref = rest
        for h in range(N_HEADS):
            qh_ref[0, h] = q[:, h * HEAD_DIM:(h + 1) * HEAD_DIM].astype(qh_ref.dtype)
        for k in range(N_KV_KINDS * KV_GROUPS):
            kvh_ref[0, k] = kv[:, k * HEAD_DIM:(k + 1) * HEAD_DIM].astype(kvh_ref.dtype)


def _nsa_proj_weights(w_in):
    n_pad = NSA_Q_COLS + NSA_KV_COLS + LANES
    return jnp.pad(w_in, ((0, 0), (0, 0), (0, n_pad - w_in.shape[-1]))).astype(jnp.bfloat16)


def _nsa_proj(x, w, layer, *, n_seq, head_major):
    R, D = x.shape
    T = R // n_seq
    tm = min(NSA_PROJ_TM, T)
    nt = T // tm
    row = lambda b, t: (b * nt + t, 0)
    out_specs = [pl.BlockSpec((tm, NSA_Q_COLS), row), pl.BlockSpec((tm, NSA_KV_COLS), row),
                 pl.BlockSpec((tm, LANES), row)]
    out_shape = [jax.ShapeDtypeStruct((R, NSA_Q_COLS), jnp.bfloat16),
                 jax.ShapeDtypeStruct((R, NSA_KV_COLS), jnp.float32),
                 jax.ShapeDtypeStruct((R, LANES), jnp.float32)]
    if head_major:
        hm = lambda b, t: (b, 0, t, 0)
        out_specs += [pl.BlockSpec((1, N_HEADS, tm, HEAD_DIM), hm),
                      pl.BlockSpec((1, N_KV_KINDS * KV_GROUPS, tm, HEAD_DIM), hm)]
        out_shape += [jax.ShapeDtypeStruct((n_seq, N_HEADS, T, HEAD_DIM), jnp.bfloat16),
                      jax.ShapeDtypeStruct((n_seq, N_KV_KINDS * KV_GROUPS, T, HEAD_DIM), jnp.bfloat16)]
    return pl.pallas_call(
        functools.partial(_nsa_proj_body, head_major=head_major),
        grid=(n_seq, nt),
        in_specs=[pl.BlockSpec((tm, D), row), _layer_spec(w, layer)],
        out_specs=out_specs,
        out_shape=out_shape,
        compiler_params=pltpu.CompilerParams(
            dimension_semantics=("arbitrary", "arbitrary"),
            vmem_limit_bytes=VMEM_LIMIT_BYTES),
        name="nsa_proj",
    )(x, w)


N_CHUNKS = 128
CMP_HALF = CMP_BLOCK // CMP_STRIDE


def _compress_rows(rows_ref, wp_ref, cb, w2_ref):
    n = N_CHUNKS
    hid = w2_ref.shape[0]
    lane = lax.broadcasted_iota(jnp.int32, (n, LANES), 1)
    low = lane < HEAD_DIM

    def tap_pair(j):
        va = rows_ref[pl.ds(2 * j, n, stride=CMP_STRIDE), :]
        vb = rows_ref[pl.ds(2 * j + 1, n, stride=CMP_STRIDE), :]
        g0 = jnp.where(low, va, pltpu.roll(vb, HEAD_DIM, 1))
        g1 = jnp.where(low, pltpu.roll(va, HEAD_DIM, 1), vb)
        return jnp.concatenate([g0, g1], axis=0).astype(jnp.bfloat16)

    z = jnp.zeros((KV_GROUPS * n, 2 * hid), jnp.float32)
    for j in range(0, CMP_STRIDE // 2, 2):
        lhs = jnp.concatenate([tap_pair(j), tap_pair(j + 1)], axis=1)
        w = jnp.concatenate([wp_ref[j], wp_ref[j + 1]], axis=0)
        z = z + jnp.dot(lhs, w, preferred_element_type=jnp.float32)
    za = z[:, :hid]
    zb = pltpu.roll(z[:, hid:], KV_GROUPS * n - 1, 0)
    h = jax.nn.gelu(za + zb + cb)
    return jnp.dot(h.astype(jnp.bfloat16), w2_ref[...], preferred_element_type=jnp.float32)


def _cmp_bias_init(cb_ref, pe_ref, w1_ref, b1_ref):
    @pl.when(pl.program_id(0) == 0)
    def _():
        for kind in range(2):
            cb = b1_ref[kind] + jnp.sum(pe_ref[kind] * w1_ref[kind], axis=0, keepdims=True)
            cb_ref[kind] = jnp.broadcast_to(cb, cb_ref.shape[1:])


CMP_WEIGHTS = ('wp', 'pe', 'w1', 'b1', 'w2')


def _cmp_weights(pe, w1, b1, w2):
    a, kinds, L, hd, hid = w1.shape
    half = L // 2
    wa = w1[:, :, :half].reshape(a, kinds, half // 2, 2 * hd, hid)
    wb = w1[:, :, half:].reshape(a, kinds, half // 2, 2 * hd, hid)
    return dict(wp=jnp.concatenate([wa, wb], axis=-1).astype(jnp.bfloat16),
                pe=pe.reshape(a, kinds, L * hd, 1), w1=w1.reshape(a, kinds, L * hd, hid),
                b1=b1.reshape(a, kinds, 1, hid), w2=w2.astype(jnp.bfloat16))


def _cmp_prompt_body(krows_ref, vrows_ref, wp_ref, pe_ref, w1_ref, b1_ref, w2_ref, out_ref, cb_ref):
    _cmp_bias_init(cb_ref, pe_ref, w1_ref, b1_ref)
    for kind, rows_ref in enumerate((krows_ref, vrows_ref)):
        res = _compress_rows(rows_ref, wp_ref.at[kind], cb_ref[kind, 0:1, :], w2_ref.at[kind])
        for g in range(KV_GROUPS):
            out_ref[0, kind, g] = res[g * N_CHUNKS:(g + 1) * N_CHUNKS, :].astype(out_ref.dtype)


def _cmp_prompt(kv, cw, *, n_seq):
    R = kv.shape[0]
    T = R // n_seq
    assert T == N_CHUNKS * CMP_STRIDE
    return pl.pallas_call(
        _cmp_prompt_body,
        grid=(n_seq,),
        in_specs=[pl.BlockSpec((T, LANES), lambda b: (b, 0)), pl.BlockSpec((T, LANES), lambda b: (b, 1))]
        + [_layer_spec(*cw[k]) for k in CMP_WEIGHTS],
        out_specs=pl.BlockSpec((1, 2, KV_GROUPS, N_CHUNKS, HEAD_DIM), lambda b: (b, 0, 0, 0, 0)),
        out_shape=jax.ShapeDtypeStruct((n_seq, 2, KV_GROUPS, N_CHUNKS, HEAD_DIM), jnp.bfloat16),
        scratch_shapes=[pltpu.VMEM((2, SUBLANES, cw['w2'][0].shape[-2]), jnp.float32)],
        compiler_params=pltpu.CompilerParams(
            dimension_semantics=("arbitrary",), vmem_limit_bytes=VMEM_LIMIT_BYTES),
        name="nsa_compress",
    )(kv, kv, *[cw[k][0] for k in CMP_WEIGHTS])


ATTN_TQ = 128
SEL_KC = 512
K_SEL, V_SEL, K_WIN, V_WIN = 2, 3, 4, 5


def _split3(x):
    hi = x.astype(jnp.bfloat16)
    r = x - hi.astype(jnp.float32)
    mid = r.astype(jnp.bfloat16)
    lo = (r - mid.astype(jnp.float32)).astype(jnp.bfloat16)
    return hi, mid, lo


def _overlap_matrix():
    c = np.arange(LANES)[:, None]
    n = np.arange(LANES)[None, :]
    ov = ((c * CMP_STRIDE < n * SEL_BLOCK + SEL_BLOCK) & (c * CMP_STRIDE + CMP_BLOCK - 1 >= n * SEL_BLOCK)
          & (c < N_CHUNKS - 1))
    return jnp.asarray(ov, jnp.bfloat16)


def _expand_matrix(n_keys):
    return jnp.asarray(np.arange(LANES)[:, None] == (np.arange(n_keys) // SEL_BLOCK)[None, :], jnp.bfloat16)


def _cmp_branch(qg, kc, vc, cmask, ov, hpg, tq, token_major=False):
    rows = hpg * tq
    s = lax.dot_general(qg, kc, (((1,), (1,)), ((), ())), preferred_element_type=jnp.float32)
    if token_major:
        shape3, cm3, head_axis = (tq, hpg, LANES), cmask[:, None, :], 1
    else:
        shape3, cm3, head_axis = (hpg, tq, LANES), cmask[None], 0
    s3 = jnp.where(cm3, s.reshape(shape3), NEG_INF)
    m = jnp.max(s3, axis=-1, keepdims=True)
    e = jnp.where(cm3, jnp.exp(s3 - m), 0.0)
    p3 = e / jnp.maximum(jnp.sum(e, axis=-1, keepdims=True), 1e-30)
    o_c = jnp.dot(p3.reshape(rows, LANES).astype(jnp.bfloat16), vc, preferred_element_type=jnp.float32)
    psum = jnp.sum(p3, axis=head_axis)
    if tq < SUBLANES:
        psum = jnp.concatenate([psum, jnp.zeros((SUBLANES - tq, LANES), jnp.float32)], axis=0)
    hi, mid, lo = _split3(psum)
    imp = (jnp.dot(hi, ov, preferred_element_type=jnp.float32)
           + jnp.dot(mid, ov, preferred_element_type=jnp.float32)
           + jnp.dot(lo, ov, preferred_element_type=jnp.float32))
    return o_c, imp


def _block_select(imp, t_col, ns):
    blk = lax.broadcasted_iota(jnp.int32, imp.shape, 1)
    cur = t_col // SEL_BLOCK
    imp = jnp.where((blk == 0) | (blk == cur) | (blk == cur - 1), FORCE_SCORE, imp)
    imp = jnp.where(blk > cur, -1.0, imp)
    imp = jnp.where(blk >= ns, -2.0, imp)
    rank = jnp.zeros(imp.shape, jnp.int32)
    for mblk in range(ns):
        col = imp[:, mblk:mblk + 1]
        beats = (col > imp) | ((col == imp) & (blk > mblk))
        rank = rank + beats.astype(jnp.int32)
    return rank < min(SEL_TOPK, ns)


def _block_select_t(imp, t_row, ns):
    ns_pad = -(-ns // SUBLANES) * SUBLANES
    it = imp.T[:ns_pad]
    blk = lax.broadcasted_iota(jnp.int32, it.shape, 0)
    cur = t_row // SEL_BLOCK
    it = jnp.where((blk == 0) | (blk == cur) | (blk == cur - 1), FORCE_SCORE, it)
    it = jnp.where(blk > cur, -1.0, it)
    it = jnp.where(blk >= ns, -2.0, it)
    rank = jnp.zeros(it.shape, jnp.int32)
    for mblk in range(ns):
        row = it[mblk:mblk + 1, :]
        beats = (row > it) | ((row == it) & (blk > mblk))
        rank = rank + beats.astype(jnp.int32)
    chosen_t = jnp.where(rank < min(SEL_TOPK, ns), 1.0, 0.0)
    chosen_t = jnp.concatenate([chosen_t, jnp.zeros((LANES - ns_pad, LANES), jnp.float32)], axis=0)
    return chosen_t.T


def _gate_col(g_ref, g, j, hpg):
    cols = [(g * hpg + h) * 3 + j for h in range(hpg)]
    return jnp.concatenate([g_ref[:, c:c + 1] for c in cols], axis=0)


def _attn_prompt_body(q_ref, kvh_ref, cmp_ref, g_ref, ov_ref, e_ref, o_ref, bias_ref):
    tq = q_ref.shape[2]
    T = kvh_ref.shape[2]
    hpg = HEADS_PER_GROUP
    rows = hpg * tq
    wk = WINDOW + tq
    t0 = pl.program_id(1) * tq
    t_col = t0 + lax.broadcasted_iota(jnp.int32, (tq, 1), 0)

    ws = pl.multiple_of(jnp.maximum(t0 - WINDOW, 0), tq)
    dist = t_col - (ws + lax.broadcasted_iota(jnp.int32, (1, wk), 1))
    bias_w = jnp.where((dist >= 0) & (dist <= WINDOW), 0.0, NEG_INF)
    cblk = lax.broadcasted_iota(jnp.int32, (1, LANES), 1)
    cmask = (cblk * CMP_STRIDE + (CMP_BLOCK - 1) <= t_col) & (cblk < N_CHUNKS - 1)
    kpos = lax.broadcasted_iota(jnp.int32, (1, T), 1)
    n_kc = (t0 + tq + SEL_KC - 1) // SEL_KC
    ov = ov_ref[...]
    ns = T // SEL_BLOCK
    t_row = t0 + lax.broadcasted_iota(jnp.int32, (1, tq), 1)
    causal_blocks = jnp.where(lax.broadcasted_iota(jnp.int32, (tq, LANES), 1) <= t_col // SEL_BLOCK, 1.0, 0.0)

    for g in range(KV_GROUPS):
        qg = q_ref[0, g * hpg:(g + 1) * hpg].reshape(rows, HEAD_DIM)
        o_c, imp = _cmp_branch(qg, cmp_ref[0, 0, g], cmp_ref[0, 1, g], cmask, ov, hpg, tq)
        if tq == LANES:
            chosen = lax.cond((t0 + tq - 1) // SEL_BLOCK + 1 > min(SEL_TOPK, ns),
                              lambda imp=imp: _block_select_t(imp, t_row, ns),
                              lambda: causal_blocks)
        else:
            chosen = jnp.where(_block_select(imp, t_col, ns), 1.0, 0.0)
        selk = jnp.dot(chosen.astype(jnp.bfloat16), e_ref[...], preferred_element_type=jnp.float32)
        bias = jnp.where((selk > 0.5) & (kpos <= t_col), 0.0, NEG_INF)
        for c in range(T // SEL_KC):
            bias_ref[c] = bias[:, c * SEL_KC:(c + 1) * SEL_KC]

        def chunk(c, carry, g=g, qg=qg):
            m, l, acc = carry
            k0 = pl.multiple_of(c * SEL_KC, SEL_KC)
            k = kvh_ref[0, K_SEL * KV_GROUPS + g, pl.ds(k0, SEL_KC), :]
            v = kvh_ref[0, V_SEL * KV_GROUPS + g, pl.ds(k0, SEL_KC), :]
            s = lax.dot_general(qg, k, (((1,), (1,)), ((), ())), preferred_element_type=jnp.float32)
            s3 = s.reshape(hpg, tq, SEL_KC) + bias_ref[c][None]
            m_new = jnp.maximum(m, jnp.max(s3, axis=-1, keepdims=True))
            alpha = jnp.exp(m - m_new)
            p = jnp.exp(s3 - m_new)
            l = alpha * l + jnp.sum(p, axis=-1, keepdims=True)
            pv = jnp.dot(p.reshape(rows, SEL_KC).astype(jnp.bfloat16), v,
                         preferred_element_type=jnp.float32)
            return m_new, l, alpha.reshape(rows, 1) * acc + pv

        m, l, acc = lax.fori_loop(
            0, n_kc, chunk,
            (jnp.full((hpg, tq, 1), NEG_INF, jnp.float32), jnp.zeros((hpg, tq, 1), jnp.float32),
             jnp.zeros((rows, HEAD_DIM), jnp.float32)))
        o_s = acc / l.reshape(rows, 1)

        kw = kvh_ref[0, K_WIN * KV_GROUPS + g, pl.ds(ws, wk), :]
        vw = kvh_ref[0, V_WIN * KV_GROUPS + g, pl.ds(ws, wk), :]
        s = lax.dot_general(qg, kw, (((1,), (1,)), ((), ())), preferred_element_type=jnp.float32)
        s3 = s.reshape(hpg, tq, wk) + bias_w[None]
        p = jnp.exp(s3 - jnp.max(s3, axis=-1, keepdims=True))
        lw = jnp.sum(p, axis=-1, keepdims=True)
        o_w = jnp.dot(p.reshape(rows, wk).astype(jnp.bfloat16), vw,
                      preferred_element_type=jnp.float32) / lw.reshape(rows, 1)

        o = (_gate_col(g_ref, g, 0, hpg) * o_c + _gate_col(g_ref, g, 1, hpg) * o_s
             + _gate_col(g_ref, g, 2, hpg) * o_w)
        for h in range(hpg):
            c0 = (g * hpg + h) * HEAD_DIM
            o_ref[:, c0:c0 + HEAD_DIM] = o[h * tq:(h + 1) * tq, :].astype(o_ref.dtype)


def _attn_prompt(qh, kvh, cmp, gates):
    B, _, T, _ = qh.shape
    tq = min(ATTN_TQ, T)
    nq = T // tq
    ov = _overlap_matrix()
    ex = _expand_matrix(T)
    return pl.pallas_call(
        _attn_prompt_body,
        grid=(B, nq),
        in_specs=[pl.BlockSpec((1, N_HEADS, tq, HEAD_DIM), lambda b, i: (b, 0, i, 0)),
                  pl.BlockSpec((1, N_KV_KINDS * KV_GROUPS, T, HEAD_DIM), lambda b, i: (b, 0, 0, 0)),
                  pl.BlockSpec((1, 2, KV_GROUPS, N_CHUNKS, HEAD_DIM), lambda b, i: (b, 0, 0, 0, 0)),
                  pl.BlockSpec((tq, LANES), lambda b, i: (b * nq + i, 0)),
                  _const_spec(ov.shape), _const_spec(ex.shape)],
        out_specs=pl.BlockSpec((tq, NSA_Q_COLS), lambda b, i: (b * nq + i, 0)),
        out_shape=jax.ShapeDtypeStruct((B * T, NSA_Q_COLS), jnp.bfloat16),
        scratch_shapes=[pltpu.VMEM((T // SEL_KC, tq, SEL_KC), jnp.float32)],
        compiler_params=pltpu.CompilerParams(
            dimension_semantics=("arbitrary", "arbitrary"),
            vmem_limit_bytes=VMEM_LIMIT_BYTES),
        name="nsa_attn_prompt",
    )(qh, kvh, cmp, gates, ov, ex)


def _nsa_prompt_pallas(xp, w_in, cw, layer):
    B, T, D = xp.shape
    _, kv, gates, qh, kvh = _nsa_proj(xp.reshape(B * T, D), w_in, layer, n_seq=B, head_major=True)
    cmp = _cmp_prompt(kv, _at_layer(cw, layer), n_seq=B)
    o = _attn_prompt(qh, kvh, cmp, gates)
    kv6 = kv.reshape(B, T, N_KV_KINDS, KV_GROUPS, HEAD_DIM)
    return o.reshape(B, T, NSA_Q_COLS), kv6[:, :, :4], kv6[:, T - min(WINDOW, T):, 4:6]


_NT = (((1,), (1,)), ((), ()))


def _softmax_pv_t(qg, kts, vts, k_new, v_new, bias):
    f32 = jnp.float32
    s = [jnp.dot(qg, kt, preferred_element_type=f32) for kt in kts]
    s.append(lax.dot_general(qg, k_new, _NT, preferred_element_type=f32))
    s = jnp.concatenate(s, axis=1) + bias
    p = jnp.exp(s - jnp.max(s, axis=-1, keepdims=True))
    l = jnp.sum(p, axis=-1, keepdims=True)
    pb = p.astype(jnp.bfloat16)
    off = 0
    acc = None
    for vt in vts:
        n = vt.shape[1]
        part = lax.dot_general(pb[:, off:off + n], vt, _NT, preferred_element_type=f32)
        acc = part if acc is None else acc + part
        off += n
    acc = acc + jnp.dot(pb[:, off:], v_new, preferred_element_type=f32)
    return acc / l


def _rep_rows(x, n_tok, hpg):
    return jnp.concatenate([jnp.broadcast_to(x[s:s + 1], (hpg, x.shape[1])) for s in range(n_tok)], axis=0)


def _attn_sample_body(pt_ref, *refs, n_pages, n_tok):
    pages = refs[:n_pages]
    (win_ref, q_ref, kvn_ref, g_ref, wp_ref, pe_ref, w1_ref, b1_ref, w2_ref, ov_ref, e_ref,
     o_ref, kc_rows, vc_rows, new_rows, cb_ref) = refs[n_pages:]
    del pt_ref
    hpg = HEADS_PER_GROUP
    page = pages[0].shape[-1]
    past = n_pages * page
    wb = win_ref.shape[-1]
    nk = past + LANES
    nw = wb + LANES
    hd = HEAD_DIM

    @pl.when(pl.program_id(0) == 0)
    def _():
        new_rows[...] = jnp.zeros(new_rows.shape, new_rows.dtype)

    bf = jnp.bfloat16
    for p in range(n_pages):
        r = slice(p * page, (p + 1) * page)
        kc_rows[r, :] = pages[p][0, 0, 0].T
        vc_rows[r, :] = pages[p][0, 0, 1].T
    kvn = kvn_ref[0]
    for i, kind in enumerate((K_SEL, V_SEL, K_WIN, V_WIN)):
        for g in range(KV_GROUPS):
            c0 = (kind * KV_GROUPS + g) * hd
            new_rows[i, g, 0:n_tok, :] = kvn[:, c0:c0 + hd].astype(bf)

    _cmp_bias_init(cb_ref, pe_ref, w1_ref, b1_ref)
    res = []
    for kind, rows_ref in enumerate((kc_rows, vc_rows)):
        res.append(_compress_rows(rows_ref, wp_ref.at[kind], cb_ref[kind, 0:1, :], w2_ref.at[kind]).astype(bf))

    t_col = past + lax.broadcasted_iota(jnp.int32, (SUBLANES, 1), 0)
    cblk = lax.broadcasted_iota(jnp.int32, (1, LANES), 1)
    cmask = ((cblk * CMP_STRIDE + (CMP_BLOCK - 1) <= t_col) & (cblk < N_CHUNKS - 1))[:n_tok]
    kpos = lax.broadcasted_iota(jnp.int32, (1, nk), 1)
    wj = lax.broadcasted_iota(jnp.int32, (1, nw), 1)
    w_pos = past - wb + wj
    dist = t_col - w_pos
    bias_w = jnp.where((dist >= 0) & (dist <= WINDOW) & (w_pos >= 0) & (wj < wb + n_tok), 0.0, NEG_INF)
    bias_w = _rep_rows(bias_w, n_tok, hpg)
    ns = -(-(past + n_tok) // SEL_BLOCK)

    for g in range(KV_GROUPS):
        qg = q_ref[0, g]
        kc = res[0][g * N_CHUNKS:(g + 1) * N_CHUNKS, :]
        vc = res[1][g * N_CHUNKS:(g + 1) * N_CHUNKS, :]
        o_c, imp = _cmp_branch(qg, kc, vc, cmask, ov_ref[...], hpg, n_tok, token_major=True)
        chosen = _block_select(imp, t_col, ns)
        selm = jnp.dot(jnp.where(chosen, 1.0, 0.0).astype(bf), e_ref[...], preferred_element_type=jnp.float32)
        bias_s = _rep_rows(jnp.where((selm > 0.5) & (kpos <= t_col), 0.0, NEG_INF), n_tok, hpg)
        gd = slice(g * hd, (g + 1) * hd)
        o_s = _softmax_pv_t(qg, [pages[p][0, 0, K_SEL, gd, :].astype(bf) for p in range(n_pages)],
                            [pages[p][0, 0, V_SEL, gd, :].astype(bf) for p in range(n_pages)],
                            new_rows[0, g], new_rows[1, g], bias_s)
        o_w = _softmax_pv_t(qg, [win_ref[0, 0, 0, gd, :].astype(bf)], [win_ref[0, 0, 1, gd, :].astype(bf)],
                            new_rows[2, g], new_rows[3, g], bias_w)
        gt = g_ref[0, g]
        o_ref[0, g] = gt[:, 0:1] * o_c + gt[:, 1:2] * o_s + gt[:, 2:3] * o_w


def _attn_sample(pool, layer, page_table, win, q, kvn, gates, cw):
    B, n_pages = page_table.shape
    page = pool.shape[-1]
    n_tok = kvn.shape[1]
    rows = q.shape[2]
    past = n_pages * page
    assert past == N_CHUNKS * CMP_STRIDE and page == LANES
    wb = win.shape[-1]
    ov = _overlap_matrix()
    ex = _expand_matrix(past + LANES)
    cwl = _at_layer(cw, layer)
    page_specs = [pl.BlockSpec((1, 1) + pool.shape[2:], lambda b, pt, p=p: (layer, pt[b, p], 0, 0, 0))
                  for p in range(n_pages)]
    grid_spec = pltpu.PrefetchScalarGridSpec(
        num_scalar_prefetch=1,
        grid=(B,),
        in_specs=page_specs + [
            pl.BlockSpec((1, 1) + win.shape[2:], lambda b, pt: (layer, b, 0, 0, 0)),
            pl.BlockSpec((1, KV_GROUPS, rows, HEAD_DIM), lambda b, pt: (b, 0, 0, 0)),
            pl.BlockSpec((1, n_tok, kvn.shape[2]), lambda b, pt: (b, 0, 0)),
            pl.BlockSpec((1, KV_GROUPS, rows, gates.shape[3]), lambda b, pt: (b, 0, 0, 0)),
        ] + [_layer_spec(*cwl[k]) for k in CMP_WEIGHTS] + [_const_spec(ov.shape), _const_spec(ex.shape)],
        out_specs=pl.BlockSpec((1, KV_GROUPS, rows, HEAD_DIM), lambda b, pt: (b, 0, 0, 0)),
        scratch_shapes=[pltpu.VMEM((past, LANES), jnp.float32), pltpu.VMEM((past, LANES), jnp.float32),
                        pltpu.VMEM((4, KV_GROUPS, LANES, HEAD_DIM), jnp.bfloat16),
                        pltpu.VMEM((2, SUBLANES, cw['w2'].shape[-2]), jnp.float32)])
    return pl.pallas_call(
        functools.partial(_attn_sample_body, n_pages=n_pages, n_tok=n_tok),
        grid_spec=grid_spec,
        out_shape=jax.ShapeDtypeStruct((B, KV_GROUPS, rows, HEAD_DIM), jnp.float32),
        compiler_params=pltpu.CompilerParams(
            dimension_semantics=("arbitrary",), vmem_limit_bytes=VMEM_LIMIT_BYTES),
        name="nsa_attn_sample",
    )(page_table, *([pool] * n_pages), win, q, kvn, gates, *[cw[k] for k in CMP_WEIGHTS], ov, ex)


def _keys_on_lanes(cache):
    a, n, rows, kinds, g, hd = cache.shape
    return cache.transpose(0, 1, 3, 4, 5, 2).reshape(a, n, kinds, g * hd, rows)


def _nsa_sample_pallas(xs, pool_t, page_table, win_t, w_in, cw, layer):
    S, B, D = xs.shape
    hpg = HEADS_PER_GROUP
    q, kv, gates = _nsa_proj(xs.reshape(S * B, D), w_in, layer, n_seq=1, head_major=False)
    q_b = q.reshape(S, B, KV_GROUPS, hpg, HEAD_DIM).transpose(1, 2, 0, 3, 4).reshape(B, KV_GROUPS, S * hpg, HEAD_DIM)
    g_b = gates[:, :GATE_COLS].reshape(S, B, KV_GROUPS, hpg, 3).transpose(1, 2, 0, 3, 4)
    g_b = jnp.pad(g_b.reshape(B, KV_GROUPS, S * hpg, 3), ((0, 0), (0, 0), (0, 0), (0, SUBLANES - 3)))
    kvn = kv.reshape(S, B, NSA_KV_COLS).swapaxes(0, 1)
    o = _attn_sample(pool_t, layer, page_table, win_t, q_b, kvn, g_b, cw)
    o = o.reshape(B, KV_GROUPS, S, hpg, HEAD_DIM).transpose(2, 0, 1, 3, 4).reshape(S, B, NSA_Q_COLS)
    kv6 = kvn.reshape(B, S, N_KV_KINDS, KV_GROUPS, HEAD_DIM)
    return o.astype(jnp.bfloat16), kv6[:, :, :4], kv6[:, :, 4:6]


def kernel(x_prompt, x_sample, cache_nsa_kv, cache_nsa_win, state_rglru_h, state_rglru_conv,
           state_ffn_conv, page_table, p_prompt, p_sample,
           nsa_w_in, nsa_w_out, nsa_cmp_pe, nsa_cmp_w1, nsa_cmp_b1, nsa_cmp_w2,
           rg_w_in, rg_conv_w, rg_conv_b, rg_gate_a_w, rg_gate_a_b, rg_gate_x_w, rg_gate_x_b,
           rg_lambda, rg_w_out,
           ffn_w_up, ffn_conv_w, ffn_conv_b, ffn_w_down,
           ln_mix_g, ln_mix_b, ln_ffn_g, ln_ffn_b, ple_w_proj, ple_w_gate):
    xp = x_prompt
    xs = x_sample.swapaxes(0, 1)
    n_tok = x_sample.shape[1]
    pool_t = _keys_on_lanes(cache_nsa_kv)
    win_t = _keys_on_lanes(cache_nsa_win)
    bf = jnp.bfloat16
    nsa_in = _nsa_proj_weights(nsa_w_in)
    cmp_w = _cmp_weights(nsa_cmp_pe, nsa_cmp_w1, nsa_cmp_b1, nsa_cmp_w2)
    rg_w = _rg_weights(rg_w_in, rg_conv_w, rg_conv_b, rg_gate_a_w, rg_gate_a_b, rg_gate_x_w, rg_gate_x_b,
                       rg_lambda)
    tail_w = _tail_weights(ffn_w_up, ffn_conv_w, ffn_conv_b, ffn_w_down, ln_mix_g, ln_mix_b, ln_ffn_g,
                           ln_ffn_b, ple_w_proj, ple_w_gate)
    mixer_out = (nsa_w_out.astype(bf), rg_w_out.astype(bf))
    p_sample_t = p_sample.swapaxes(1, 2)
    kv_p, kv_s, win_p, win_s = [], [], [], []
    h_p, h_s, rc_p, rc_s = [], [], [], []
    fc_p, fc_s = [], []
    for i in range(DEPTH):
        j = i // N_MIXERS
        if i % N_MIXERS == 0:
            o_p, rows_p, wnd_p = _nsa_prompt_pallas(xp, nsa_in, cmp_w, j)
            o_s, rows_s, wnd_s = _nsa_sample_pallas(xs, pool_t, page_table, win_t, nsa_in, cmp_w, j)
            kv_p.append(rows_p)
            kv_s.append(rows_s)
            win_p.append(wnd_p)
            win_s.append(wnd_s)
        else:
            rgw = _at_layer(rg_w, j)
            o_p, hl_p, cs_p = _rglru_prompt(xp, rgw)
            o_s, hl_s, cs_s = _rglru_sample(xs, state_rglru_h[j], state_rglru_conv[j], rgw)
            h_p.append(hl_p)
            h_s.append(hl_s)
            rc_p.append(cs_p)
            rc_s.append(cs_s)
        w = dict(_at_layer(tail_w, i), out=(mixer_out[i % N_MIXERS], j))
        xp, fs_p = _layer_tail_prompt(xp, o_p, p_prompt[i], w)
        xs, fs_s = _layer_tail_sample(xs, o_s, p_sample_t[i], state_ffn_conv[i], w)
        fc_p.append(fs_p)
        fc_s.append(fs_s)
    win_sample = jnp.concatenate([cache_nsa_win[:, :, n_tok:], jnp.stack(win_s)], axis=2)
    return (xp, xs.swapaxes(0, 1),
            jnp.stack(kv_p), jnp.stack(kv_s),
            jnp.stack(win_p), win_sample,
            jnp.stack(h_p), jnp.stack(h_s),
            jnp.stack(rc_p), jnp.stack(rc_s),
            jnp.stack(fc_p), jnp.stack(fc_s))
```
